```python
import math, functools
import jax, jax.numpy as jnp
from jax import lax
import numpy as np

D_MODEL = 2048
BATCH = 4
SEQ = 2048
DEPTH = 4
DEC_BATCH = 128
DEC_SEQ = 4
PAST_LEN = 8192
PAGE_SIZE = 128

MLA_HEADS = 8
Q_LORA = 512
KV_LORA = 256
QK_NOPE = 128
QK_ROPE = 64
QK_HEAD = QK_NOPE + QK_ROPE
V_HEAD = 128
MLA_WIDTH = MLA_HEADS * V_HEAD
ROPE_BASE = 10000.0
Q_BLOCK = 128
CONV_CH = 512
CONV_W = 31
GLA_HEADS = 4
GLA_DK = 64
GLA_DV = 128
GLA_RANK = 16
GLA_TAU = 16.0
GLA_CHUNK = 64
D_FF = -(-8 * D_MODEL // 768) * 256
NORM_EPS = 1e-6
LN_EPS = 1e-5
IN_SIZES = (Q_LORA, KV_LORA, QK_ROPE, 2 * CONV_CH, GLA_HEADS * GLA_DK, GLA_HEADS * GLA_DK, GLA_HEADS * GLA_DV, GLA_RANK, GLA_HEADS * GLA_DV, 3 * D_MODEL)
D_IN = sum(IN_SIZES)

kernel_name = 'hybrid_mla_conformer_gla_decode_step'


def rmsnorm(x, g):
    xf = x.astype(jnp.float32)
    y = xf * lax.rsqrt(jnp.mean(xf * xf, -1, keepdims=True) + NORM_EPS)
    return (y * g.astype(jnp.float32)).astype(x.dtype)


def layernorm(x, g, b):
    xf = x.astype(jnp.float32)
    mu = jnp.mean(xf, -1, keepdims=True)
    xc = xf - mu
    y = xc * lax.rsqrt(jnp.mean(xc * xc, -1, keepdims=True) + LN_EPS)
    return (y * g.astype(jnp.float32) + b.astype(jnp.float32)).astype(x.dtype)


def split_cols(z):
    outs, off = [], 0
    for s in IN_SIZES:
        outs.append(z[..., off:off + s])
        off += s
    return outs


def rope(x, pos):
    half = QK_ROPE // 2
    inv = jnp.exp(-math.log(ROPE_BASE) * jnp.arange(half, dtype=jnp.float32) / half)
    ang = pos.astype(jnp.float32)[:, None] * inv[None, :]
    cos = jnp.cos(ang)[:, None, :]
    sin = jnp.sin(ang)[:, None, :]
    xf = x.astype(jnp.float32)
    x1, x2 = xf[..., :half], xf[..., half:]
    return jnp.concatenate([x1 * cos - x2 * sin, x2 * cos + x1 * sin], -1).astype(x.dtype)


def mla_queries(z_cq, pos, g_cq, w_uq, g_q):
    q = jnp.einsum('...lr,rhe->...lhe', rmsnorm(z_cq, g_cq), w_uq)
    q = rmsnorm(q, g_q)
    return jnp.concatenate([q[..., :QK_NOPE], rope(q[..., QK_NOPE:], pos)], -1)


def mla_keys(c, kr, pos, w_uk, g_k):
    k_nope = jnp.einsum('...lr,rhd->...lhd', c, w_uk)
    kr_b = jnp.broadcast_to(kr[..., None, :], k_nope.shape[:-1] + (QK_ROPE,))
    k = rmsnorm(jnp.concatenate([k_nope, kr_b], -1), g_k)
    return jnp.concatenate([k[..., :QK_NOPE], rope(k[..., QK_NOPE:], pos)], -1)


def latent_attend(q, k, c, q_pos, k_pos):
    s = jnp.einsum('...qhe,...khe->...hqk', q, k).astype(jnp.float32) * (QK_HEAD ** -0.5)
    s = jnp.where(k_pos[None, :] <= q_pos[:, None], s, -jnp.inf)
    p = jax.nn.softmax(s, axis=-1).astype(c.dtype)
    return jnp.einsum('...hqk,...kr->...qhr', p, c)


def prompt_attend(q, c_kv, kr, pos, w_uk, g_k):
    k = mla_keys(c_kv, kr, pos, w_uk, g_k)
    B, L, H, E = q.shape
    blk = math.gcd(L, Q_BLOCK)
    nb = L // blk
    qb = q.reshape(B, nb, blk, H, E).transpose(1, 0, 2, 3, 4)
    pb = pos.reshape(nb, blk)
    out = lax.map(lambda a: latent_attend(a[0], k, c_kv, a[1], pos), (qb, pb))
    return out.transpose(1, 0, 2, 3, 4).reshape(B, L, H, KV_LORA)


def sample_attend(q, c_new, kr_new, cache_kv, cache_kr, layer, page_table, w_uk, g_k):
    Q = q.shape[1]
    past = page_table.shape[1] * cache_kv.shape[2]
    q_pos = past + jnp.arange(Q, dtype=jnp.int32)
    k_pos = jnp.arange(past + Q, dtype=jnp.int32)

    def one(args):
        qb, cb, krb, rows = args
        c_all = jnp.concatenate([cache_kv[layer, rows].reshape(past, KV_LORA), cb], 0)
        kr_all = jnp.concatenate([cache_kr[layer, rows].reshape(past, QK_ROPE), krb], 0)
        k_all = mla_keys(c_all, kr_all, k_pos, w_uk, g_k)
        return latent_attend(qb, k_all, c_all, q_pos, k_pos)

    return lax.map(one, (q, c_new, kr_new, page_table))


def causal_dwconv(u, buf, w, b):
    xp = jnp.concatenate([buf.astype(u.dtype), u], 1)
    y = lax.conv_general_dilated(xp, w[:, None, :].astype(u.dtype), (1,), 'VALID',
                                 dimension_numbers=('NWC', 'WIO', 'NWC'), feature_group_count=CONV_CH)
    return y + b, xp[:, -(CONV_W - 1):]


def gla_chunked(q, k, v, log_a, S0):
    B, L, H, _ = q.shape
    C = math.gcd(L, GLA_CHUNK)
    n = L // C

    def to_chunks(t):
        return t.astype(jnp.float32).reshape(B, n, C, H, t.shape[-1]).transpose(1, 0, 3, 2, 4)

    qs, ks, vs, als = to_chunks(q), to_chunks(k), to_chunks(v), to_chunks(log_a)
    causal = jnp.tril(jnp.ones((C, C), dtype=bool))

    def step(S, inp):
        qc, kc, vc, ac = inp
        b = jnp.cumsum(ac, axis=-2)
        diff = jnp.where(causal[:, :, None], b[..., :, None, :] - b[..., None, :, :], -jnp.inf)
        A = jnp.einsum('bhid,bhjd,bhijd->bhij', qc, kc, jnp.exp(diff))
        o = jnp.einsum('bhid,bhde->bhie', qc * jnp.exp(b), S) + jnp.einsum('bhij,bhje->bhie', A, vc)
        b_last = b[..., -1:, :]
        S = jnp.exp(b[..., -1, :])[..., None] * S + jnp.einsum('bhjd,bhje->bhde', kc * jnp.exp(b_last - b), vc)
        return S, o

    S, o = lax.scan(step, S0.astype(jnp.float32), (qs, ks, vs, als))
    o = o.transpose(1, 0, 3, 2, 4).reshape(B, L, H, v.shape[-1])
    return o.astype(q.dtype), S.astype(S0.dtype)


def trunk_layer(x, pos, conv_buf, gla_S, attend, lw):
    B, L, _ = x.shape
    h = rmsnorm(x, lw['attn_norm'])
    z = h @ lw['w_in']
    z_cq, z_ckv, z_kr, z_glu, z_gq, z_gk, z_gv, z_ga, z_gr, z_mg = split_cols(z)
    q = mla_queries(z_cq, pos, lw['cq_norm'], lw['w_uq'], lw['q_norm'])
    c_kv = rmsnorm(z_ckv, lw['kv_norm'])
    lat = attend(q, c_kv, z_kr)
    y_a = jnp.einsum('blhr,rhd->blhd', lat, lw['w_uv']).reshape(B, L, MLA_WIDTH)
    br_a = y_a @ lw['w_pa']
    u = z_glu[..., :CONV_CH] * jax.nn.sigmoid(z_glu[..., CONV_CH:])
    yc, conv_new = causal_dwconv(u, conv_buf, lw['conv_w'], lw['conv_b'])
    br_b = jax.nn.silu(layernorm(yc, lw['conv_ln_g'], lw['conv_ln_b'])) @ lw['w_pb']
    gq = z_gq.reshape(B, L, GLA_HEADS, GLA_DK) * (GLA_DK ** -0.5)
    gk = z_gk.reshape(B, L, GLA_HEADS, GLA_DK)
    gv = z_gv.reshape(B, L, GLA_HEADS, GLA_DV)
    log_a = jax.nn.log_sigmoid((z_ga @ lw['gla_wa'] + lw['gla_ba']).astype(jnp.float32)) / GLA_TAU
    o, S_new = gla_chunked(gq, gk, gv, log_a.reshape(B, L, GLA_HEADS, GLA_DK), gla_S)
    y_c = (rmsnorm(o, lw['gla_norm']) * jax.nn.silu(z_gr.reshape(B, L, GLA_HEADS, GLA_DV))).reshape(B, L, GLA_HEADS * GLA_DV)
    br_c = y_c @ lw['w_pc']
    g = jax.nn.sigmoid(z_mg).reshape(B, L, 3, D_MODEL)
    m = g[..., 0, :] * br_a + g[..., 1, :] * br_b + g[..., 2, :] * br_c
    x = x + m @ lw['w_o']
    h2 = rmsnorm(x, lw['ffn_norm'])
    gu = h2 @ lw['w_ffn_in']
    x = x + (jax.nn.silu(gu[..., :D_FF]) * gu[..., D_FF:]) @ lw['w_ffn_out']
    return x, c_kv, z_kr, conv_new, S_new


def setup_inputs(seed: int = 0) -> dict:
    key = jax.random.key(seed)
    ks = iter(jax.random.split(key, 40))
    f32 = jnp.float32

    def nrm(shape, scale):
        return jax.random.normal(next(ks), shape, f32) * scale

    def gain(shape):
        return 1.0 + nrm(shape, 0.02)

    n_pages = PAST_LEN // PAGE_SIZE
    n_used = DEC_BATCH * n_pages
    n_pool = (n_used * 5 + 3) // 4
    x_prompt = nrm((BATCH, SEQ, D_MODEL), 1.0)
    x_sample = nrm((DEC_BATCH, DEC_SEQ, D_MODEL), 1.0)
    cache_kv = nrm((DEPTH, n_pool, PAGE_SIZE, KV_LORA), 1.0)
    cache_krope = nrm((DEPTH, n_pool, PAGE_SIZE, QK_ROPE), 1.0)
    state_conv = nrm((DEPTH, DEC_BATCH, CONV_W - 1, CONV_CH), 0.5)
    state_gla = nrm((DEPTH, DEC_BATCH, GLA_HEADS, GLA_DK, GLA_DV), 2.0)
    page_table = jax.random.permutation(next(ks), n_pool)[:n_used].reshape(DEC_BATCH, n_pages).astype(jnp.int32)
    return {
        'x_prompt': x_prompt,
        'x_sample': x_sample,
        'cache_kv': cache_kv,
        'cache_krope': cache_krope,
        'state_conv': state_conv,
        'state_gla': state_gla,
        'page_table': page_table,
        'attn_norm': gain((DEPTH, D_MODEL)),
        'w_in': nrm((DEPTH, D_MODEL, D_IN), D_MODEL ** -0.5),
        'cq_norm': gain((DEPTH, Q_LORA)),
        'w_uq': nrm((DEPTH, Q_LORA, MLA_HEADS, QK_HEAD), Q_LORA ** -0.5),
        'q_norm': gain((DEPTH, QK_HEAD)),
        'kv_norm': gain((DEPTH, KV_LORA)),
        'w_uk': nrm((DEPTH, KV_LORA, MLA_HEADS, QK_NOPE), KV_LORA ** -0.5),
        'k_norm': gain((DEPTH, QK_HEAD)),
        'w_uv': nrm((DEPTH, KV_LORA, MLA_HEADS, V_HEAD), KV_LORA ** -0.5),
        'w_pa': nrm((DEPTH, MLA_WIDTH, D_MODEL), MLA_WIDTH ** -0.5),
        'conv_w': nrm((DEPTH, CONV_W, CONV_CH), CONV_W ** -0.5),
        'conv_b': nrm((DEPTH, CONV_CH), 0.02),
        'conv_ln_g': gain((DEPTH, CONV_CH)),
        'conv_ln_b': nrm((DEPTH, CONV_CH), 0.02),
        'w_pb': nrm((DEPTH, CONV_CH, D_MODEL), CONV_CH ** -0.5),
        'gla_wa': nrm((DEPTH, GLA_RANK, GLA_HEADS * GLA_DK), GLA_RANK ** -0.5),
        'gla_ba': nrm((DEPTH, GLA_HEADS * GLA_DK), 0.02),
        'gla_norm': gain((DEPTH, GLA_DV)),
        'w_pc': nrm((DEPTH, GLA_HEADS * GLA_DV, D_MODEL), (GLA_HEADS * GLA_DV) ** -0.5),
        'w_o': nrm((DEPTH, D_MODEL, D_MODEL), D_MODEL ** -0.5),
        'ffn_norm': gain((DEPTH, D_MODEL)),
        'w_ffn_in': nrm((DEPTH, D_MODEL, 2 * D_FF), D_MODEL ** -0.5),
        'w_ffn_out': nrm((DEPTH, D_FF, D_MODEL), D_FF ** -0.5),
    }


def reference(x_prompt, x_sample, cache_kv, cache_krope, state_conv, state_gla, page_table,
              attn_norm, w_in, cq_norm, w_uq, q_norm, kv_norm, w_uk, k_norm, w_uv, w_pa,
              conv_w, conv_b, conv_ln_g, conv_ln_b, w_pb, gla_wa, gla_ba, gla_norm, w_pc,
              w_o, ffn_norm, w_ffn_in, w_ffn_out):
    past = page_table.shape[1] * cache_kv.shape[2]
    Bp, Lp, _ = x_prompt.shape
    Ls = x_sample.shape[1]
    pos_p = jnp.arange(Lp, dtype=jnp.int32)
    pos_s = past + jnp.arange(Ls, dtype=jnp.int32)
    xp, xs = x_prompt, x_sample
    ckv_p, kr_p, conv_p, gla_p = [], [], [], []
    ckv_s, kr_s, conv_s, gla_s = [], [], [], []
    for l in range(DEPTH):
        lw = {
            'attn_norm': attn_norm[l], 'w_in': w_in[l], 'cq_norm': cq_norm[l], 'w_uq': w_uq[l],
            'q_norm': q_norm[l], 'kv_norm': kv_norm[l], 'w_uv': w_uv[l], 'w_pa': w_pa[l],
            'conv_w': conv_w[l], 'conv_b': conv_b[l], 'conv_ln_g': conv_ln_g[l], 'conv_ln_b': conv_ln_b[l],
            'w_pb': w_pb[l], 'gla_wa': gla_wa[l], 'gla_ba': gla_ba[l], 'gla_norm': gla_norm[l],
            'w_pc': w_pc[l], 'w_o': w_o[l], 'ffn_norm': ffn_norm[l], 'w_ffn_in': w_ffn_in[l],
            'w_ffn_out': w_ffn_out[l],
        }
        att_p = functools.partial(prompt_attend, pos=pos_p, w_uk=w_uk[l], g_k=k_norm[l])
        conv0 = jnp.zeros((Bp, CONV_W - 1, CONV_CH), xp.dtype)
        S0 = jnp.zeros((Bp, GLA_HEADS, GLA_DK, GLA_DV), xp.dtype)
        xp, c_new, r_new, cb_new, S_new = trunk_layer(xp, pos_p, conv0, S0, att_p, lw)
        ckv_p.append(c_new); kr_p.append(r_new); conv_p.append(cb_new); gla_p.append(S_new)
        att_s = functools.partial(sample_attend, cache_kv=cache_kv, cache_kr=cache_krope, layer=l,
                                  page_table=page_table, w_uk=w_uk[l], g_k=k_norm[l])
        xs, c_new, r_new, cb_new, S_new = trunk_layer(xs, pos_s, state_conv[l], state_gla[l], att_s, lw)
        ckv_s.append(c_new); kr_s.append(r_new); conv_s.append(cb_new); gla_s.append(S_new)
    return (xp, xs,
            jnp.stack(ckv_p), jnp.stack(kr_p), jnp.stack(conv_p), jnp.stack(gla_p),
            jnp.stack(ckv_s), jnp.stack(kr_s), jnp.stack(conv_s), jnp.stack(gla_s))
```

```python
import functools
import math

import jax
import jax.numpy as jnp
from jax import lax
from jax.experimental import pallas as pl
from jax.experimental.pallas import tpu as pltpu

F32 = jnp.float32
BF16 = jnp.bfloat16

NORM_EPS = 1e-6
LN_EPS = 1e-5
ROPE_BASE = 10000.0
GLA_TAU = 16.0

LANE = 128
SUBLANE = 8
VMEM_LIMIT = 56 * 1024 * 1024
GLA_BLOCK = 16
GLA_TILE = 128
HEAD_PAD = 256


def _cparams(*sem):
    return pltpu.CompilerParams(dimension_semantics=sem, vmem_limit_bytes=VMEM_LIMIT)


def _tile(n, pref):
    if n <= pref:
        return n
    t = pref
    while n % t:
        t -= SUBLANE
    return t


def _rms(x, g, eps=NORM_EPS):
    return x * lax.rsqrt(jnp.mean(x * x, axis=-1, keepdims=True) + eps) * g


def _dot(a, b):
    return jnp.dot(a, b, preferred_element_type=F32)


def _dot_nt(a, b):
    return lax.dot_general(a, b, (((1,), (1,)), ((), ())), preferred_element_type=F32)


def _sigmoid(x):
    return 1.0 / (1.0 + jnp.exp(-x))


def _silu(x):
    return x * _sigmoid(x)


def _rope_lanes(x, cos_t, sin_t):
    return x * cos_t + (pltpu.roll(x, 96, 1) + pltpu.roll(x, 32, 1)) * sin_t


def _norm_matmul_body(x_ref, g_ref, w_ref, o_ref, h_ref):
    @pl.when(pl.program_id(1) == 0)
    def _():
        h_ref[...] = _rms(x_ref[...], g_ref[...]).astype(BF16)

    o_ref[...] = _dot(h_ref[...], w_ref[...])


def _norm_matmul(x, g, w, layer, tm, tn):
    m, k = x.shape
    n = w.shape[-1]
    return pl.pallas_call(
        _norm_matmul_body,
        grid=(m // tm, n // tn),
        in_specs=[
            pl.BlockSpec((tm, k), lambda i, j: (i, 0)),
            pl.BlockSpec((None, 1, k), lambda i, j: (layer, 0, 0)),
            pl.BlockSpec((None, k, tn), lambda i, j: (layer, 0, j)),
        ],
        out_specs=pl.BlockSpec((tm, tn), lambda i, j: (i, j)),
        out_shape=jax.ShapeDtypeStruct((m, n), F32),
        scratch_shapes=[pltpu.VMEM((tm, k), BF16)],
        compiler_params=_cparams("parallel", "arbitrary"),
        name="in_proj",
    )(x, g, w)


def _head_norm_q(qraw, h, gq_ref, cos_t, sin_t, width):
    lo = h * HEAD_PAD
    qn = qraw[:, lo:lo + LANE]
    qr = qraw[:, lo + LANE:lo + HEAD_PAD]
    ss = jnp.sum(qn * qn, axis=-1, keepdims=True) + jnp.sum(qr * qr, axis=-1, keepdims=True)
    rinv = lax.rsqrt(ss * (1.0 / width) + NORM_EPS)
    qn = qn * rinv * gq_ref[:, :LANE]
    qr = _rope_lanes(qr * rinv * gq_ref[:, LANE:], cos_t, sin_t)
    return qn, qr


def _qk_prompt_body(zcq_ref, zckv_ref, zkr_ref, cos_ref, sin_ref, gcq_ref, wuq_ref, gq_ref, gkv_ref,
                    wuk_ref, gk_ref, q_ref, k_ref, ckv_ref, cb_ref, *, heads, width):
    cos_t = cos_ref[...]
    sin_t = sin_ref[...]
    cq = _rms(zcq_ref[...], gcq_ref[...]).astype(BF16)
    qraw = _dot(cq, wuq_ref[...])
    c = _rms(zckv_ref[...], gkv_ref[...])
    ckv_ref[...] = c
    cb = c.astype(BF16)
    cb_ref[...] = cb
    kn = _dot(cb, wuk_ref[...])
    krp = zkr_ref[...]
    kr_ss = jnp.sum(krp * krp, axis=-1, keepdims=True)
    for h in range(heads):
        qn, qr = _head_norm_q(qraw, h, gq_ref, cos_t, sin_t, width)
        lo = h * HEAD_PAD
        q_ref[:, lo:lo + LANE] = qn.astype(BF16)
        q_ref[:, lo + LANE:lo + HEAD_PAD] = qr.astype(BF16)
        kh = kn[:, h * LANE:(h + 1) * LANE]
        rk = lax.rsqrt((jnp.sum(kh * kh, axis=-1, keepdims=True) + kr_ss) * (1.0 / width) + NORM_EPS)
        k_ref[:, lo:lo + LANE] = (kh * rk * gk_ref[:, :LANE]).astype(BF16)
        k_ref[:, lo + LANE:lo + HEAD_PAD] = _rope_lanes(krp * rk * gk_ref[:, LANE:], cos_t, sin_t).astype(BF16)


def _qk_prompt(z, cols, cos_t, sin_t, wts, layer, tm, heads, width):
    m = z.shape[0]
    n_tab = cos_t.shape[0] // tm
    hp = heads * HEAD_PAD

    def zspec(name):
        off, w = cols[name]
        return pl.BlockSpec((tm, w), lambda i: (i, off // w))

    def wspec(a):
        return pl.BlockSpec((None,) + a.shape[1:], lambda i: (layer,) + (0,) * (a.ndim - 1))

    tab = pl.BlockSpec((tm, LANE), lambda i: (i % n_tab, 0))
    return pl.pallas_call(
        functools.partial(_qk_prompt_body, heads=heads, width=width),
        grid=(m // tm,),
        in_specs=[zspec("cq"), zspec("ckv"), zspec("kr"), tab, tab,
                  wspec(wts["g_cq"]), wspec(wts["w_uq"]), wspec(wts["g_q"]), wspec(wts["g_kv"]),
                  wspec(wts["w_uk"]), wspec(wts["g_k"])],
        out_specs=[pl.BlockSpec((tm, hp), lambda i: (i, 0)), pl.BlockSpec((tm, hp), lambda i: (i, 0)),
                   pl.BlockSpec((tm, cols["ckv"][1]), lambda i: (i, 0)),
                   pl.BlockSpec((tm, cols["ckv"][1]), lambda i: (i, 0))],
        out_shape=[jax.ShapeDtypeStruct((m, hp), BF16), jax.ShapeDtypeStruct((m, hp), BF16),
                   jax.ShapeDtypeStruct((m, cols["ckv"][1]), F32),
                   jax.ShapeDtypeStruct((m, cols["ckv"][1]), BF16)],
        compiler_params=_cparams("parallel"),
        name="qk_prompt",
    )(z, z, z, cos_t, sin_t, wts["g_cq"], wts["w_uq"], wts["g_q"], wts["g_kv"], wts["w_uk"], wts["g_k"])


def _q_sample_body(zcq_ref, zckv_ref, cos_ref, sin_ref, gcq_ref, wuq_ref, gq_ref, gkv_ref, wuk3_ref,
                   gk_ref, gt1_ref, gt2_ref, qlat_ref, qg_ref, ckv_ref, *, heads, width):
    cos_t = cos_ref[...]
    sin_t = sin_ref[...]
    cq = _rms(zcq_ref[...], gcq_ref[...]).astype(BF16)
    qraw = _dot(cq, wuq_ref[...])
    ckv_ref[...] = _rms(zckv_ref[...], gkv_ref[...])
    zero = jnp.zeros((qraw.shape[0], LANE), BF16)
    for h in range(heads):
        qn, qr = _head_norm_q(qraw, h, gq_ref, cos_t, sin_t, width)
        lo = h * HEAD_PAD
        qlat_ref[:, lo:lo + HEAD_PAD] = _dot_nt((qn * gk_ref[:, :LANE]).astype(BF16), wuk3_ref[h]).astype(BF16)
        feat = qr * gt1_ref[...] + (pltpu.roll(qr, 32, 1) + pltpu.roll(qr, 96, 1)) * gt2_ref[...]
        qg_ref[:, lo:lo + LANE] = feat.astype(BF16)
        qg_ref[:, lo + LANE:lo + HEAD_PAD] = zero


def _q_sample(z, cols, cos_t, sin_t, wts, layer, heads, width):
    m = z.shape[0]
    tm = m
    hp = heads * HEAD_PAD

    def zspec(name):
        off, w = cols[name]
        return pl.BlockSpec((tm, w), lambda i: (i, off // w))

    def wspec(a):
        return pl.BlockSpec((None,) + a.shape[1:], lambda i: (layer,) + (0,) * (a.ndim - 1))

    tab = pl.BlockSpec((tm, LANE), lambda i: (i, 0))
    kvw = cols["ckv"][1]
    return pl.pallas_call(
        functools.partial(_q_sample_body, heads=heads, width=width),
        grid=(m // tm,),
        in_specs=[zspec("cq"), zspec("ckv"), tab, tab,
                  wspec(wts["g_cq"]), wspec(wts["w_uq"]), wspec(wts["g_q"]), wspec(wts["g_kv"]),
                  wspec(wts["w_uk3"]), wspec(wts["g_k"]), wspec(wts["g_t1"]), wspec(wts["g_t2"])],
        out_specs=[pl.BlockSpec((tm, hp), lambda i: (i, 0)), pl.BlockSpec((tm, hp), lambda i: (i, 0)),
                   pl.BlockSpec((tm, kvw), lambda i: (i, 0))],
        out_shape=[jax.ShapeDtypeStruct((m, hp), BF16), jax.ShapeDtypeStruct((m, hp), BF16),
                   jax.ShapeDtypeStruct((m, kvw), F32)],
        compiler_params=_cparams("parallel"),
        name="q_sample",
    )(z, z, cos_t, sin_t, wts["g_cq"], wts["w_uq"], wts["g_q"], wts["g_kv"], wts["w_uk3"], wts["g_k"],
      wts["g_t1"], wts["g_t2"])


def _pattn_body(q_ref, k_ref, c_ref, wuv_ref, y_ref, m_ref, l_ref, acc_ref, *, heads, tq, tk, scale):
    qi = pl.program_id(1)
    ki = pl.program_id(2)

    @pl.when(ki == 0)
    def _():
        m_ref[...] = jnp.full(m_ref.shape, -jnp.inf, F32)
        l_ref[...] = jnp.zeros(l_ref.shape, F32)
        acc_ref[...] = jnp.zeros(acc_ref.shape, F32)

    @pl.when(ki * tk <= qi * tq + tq - 1)
    def _():
        qpos = qi * tq + lax.broadcasted_iota(jnp.int32, (tq, tk), 0)
        kpos = ki * tk + lax.broadcasted_iota(jnp.int32, (tq, tk), 1)
        visible = kpos <= qpos
        c = c_ref[...]
        for h in range(heads):
            lo = h * HEAD_PAD
            s = _dot_nt(q_ref[:, lo:lo + HEAD_PAD], k_ref[:, lo:lo + HEAD_PAD]) * scale
            s = jnp.where(visible, s, -jnp.inf)
            m_prev = m_ref[h]
            m_new = jnp.maximum(m_prev, jnp.max(s, axis=-1, keepdims=True))
            alpha = jnp.exp(m_prev - m_new)
            p = jnp.exp(s - m_new)
            l_ref[h] = alpha * l_ref[h] + jnp.sum(p, axis=-1, keepdims=True)
            acc_ref[h] = alpha * acc_ref[h] + _dot(p.astype(BF16), c)
            m_ref[h] = m_new

    @pl.when(ki == pl.num_programs(2) - 1)
    def _():
        for h in range(heads):
            lat = (acc_ref[h] / l_ref[h]).astype(BF16)
            y_ref[:, h * LANE:(h + 1) * LANE] = _dot(lat, wuv_ref[h]).astype(BF16)


def _prompt_attention(q, k, cb, w_uv, layer, batch, seq, heads, width, tq, tk):
    m = q.shape[0]
    nq, nk = seq // tq, seq // tk
    hp = heads * HEAD_PAD
    kvw = cb.shape[1]

    def kv_idx(b, i, j):
        return (b * nk + jnp.minimum(j, (i * tq + tq - 1) // tk), 0)

    return pl.pallas_call(
        functools.partial(_pattn_body, heads=heads, tq=tq, tk=tk, scale=width ** -0.5),
        grid=(batch, nq, nk),
        in_specs=[pl.BlockSpec((tq, hp), lambda b, i, j: (b * nq + i, 0)),
                  pl.BlockSpec((tk, hp), kv_idx),
                  pl.BlockSpec((tk, kvw), kv_idx),
                  pl.BlockSpec((None, heads, kvw, LANE), lambda b, i, j: (layer, 0, 0, 0))],
        out_specs=pl.BlockSpec((tq, heads * LANE), lambda b, i, j: (b * nq + i, 0)),
        out_shape=jax.ShapeDtypeStruct((m, heads * LANE), BF16),
        scratch_shapes=[pltpu.VMEM((heads, tq, 1), F32), pltpu.VMEM((heads, tq, 1), F32),
                        pltpu.VMEM((heads, tq, kvw), F32)],
        compiler_params=_cparams("parallel", "parallel", "arbitrary"),
        name="prompt_attention",
    )(q, k, cb, w_uv)


def _key_tile_scores(lhs_main, lhs_rope, cb, kr, cs, heads, width, scale):
    tk = cb.shape[0]
    nk = heads * LANE
    r = _dot_nt(lhs_main, cb)
    kn = r[:nk].reshape(heads, LANE, tk)
    ssq = jnp.sum(kn * kn, axis=1)
    xe = jnp.concatenate([kr, kr], axis=-1)
    feat = (xe * cs).astype(BF16)
    sq = xe * xe
    hi = sq.astype(BF16)
    lane = lax.broadcasted_iota(jnp.int32, (tk, LANE), 1)
    hi_f = hi.astype(F32)
    sq_split = jnp.where(lane < LANE // 2, hi_f, sq - hi_f).astype(BF16)
    r2 = _dot_nt(lhs_rope, jnp.concatenate([feat, sq_split], axis=-1))
    nq = lhs_rope.shape[0] - 2 * SUBLANE
    rinv = lax.rsqrt((ssq + r2[nq:nq + heads]) * (1.0 / width) + NORM_EPS) * scale
    s = (r[nk:] + r2[:nq]).reshape(nq // heads, heads, tk) * rinv[None]
    return s.reshape(nq, tk)


def _sattn_body(pt_ref, wt_ref, qlat_ref, qg_ref, cn_ref, krn_ref, cs_ref, wuv_ref, ckv_hbm, ckr_hbm,
                y_ref, cbuf, kbuf, sems, *, layer, heads, width, lq, n_pages, page, tk):
    s_idx = pl.program_id(0)
    n_seq = pl.num_programs(0)
    past = n_pages * page
    nq = lq * heads
    scale = width ** -0.5

    def page_copies(seq, j, slot):
        pg = pt_ref[seq, j]
        rows = pl.ds(pl.multiple_of(j * page, page), page)
        return (pltpu.make_async_copy(ckv_hbm.at[layer, pg], cbuf.at[slot, rows], sems.at[0, slot]),
                pltpu.make_async_copy(ckr_hbm.at[layer, pg], kbuf.at[slot, rows], sems.at[1, slot]))

    def start_fetch(seq, slot):
        def body(j, carry):
            for cp in page_copies(seq, j, slot):
                cp.start()
            return carry
        lax.fori_loop(0, n_pages, body, 0)

    def wait_fetch(seq, slot):
        def body(j, carry):
            for cp in page_copies(seq, j, slot):
                cp.wait()
            return carry
        lax.fori_loop(0, n_pages, body, 0)

    slot = s_idx % 2

    @pl.when(s_idx == 0)
    def _():
        start_fetch(0, 0)

    @pl.when(s_idx + 1 < n_seq)
    def _():
        start_fetch(s_idx + 1, 1 - slot)

    wait_fetch(s_idx, slot)

    lhs_main = jnp.concatenate([wt_ref[...], qlat_ref[...]], axis=0)
    tail_r = lax.broadcasted_iota(jnp.int32, (2 * SUBLANE, 2 * LANE), 0)
    tail_c = lax.broadcasted_iota(jnp.int32, (2 * SUBLANE, 2 * LANE), 1)
    ones_rows = jnp.where((tail_r < SUBLANE) & (tail_c >= LANE), 1.0, 0.0).astype(BF16)
    lhs_rope = jnp.concatenate([qg_ref[...], ones_rows], axis=0)

    def update(carry, s, cb):
        m_prev, l_prev, acc = carry
        m_new = jnp.maximum(m_prev, jnp.max(s, axis=-1, keepdims=True))
        alpha = jnp.exp(m_prev - m_new)
        p = jnp.exp(s - m_new)
        return (m_new, alpha * l_prev + jnp.sum(p, axis=-1, keepdims=True),
                alpha * acc + _dot(p.astype(BF16), cb))

    def past_tile(t, carry):
        rows = pl.ds(pl.multiple_of(t * tk, tk), tk)
        cb = cbuf[slot, rows, :].astype(BF16)
        s = _key_tile_scores(lhs_main, lhs_rope, cb, kbuf[slot, rows, :], cs_ref[rows, :],
                             heads, width, scale)
        return update(carry, s, cb)

    kvw = cbuf.shape[-1]
    carry = (jnp.full((nq, 1), -jnp.inf, F32), jnp.zeros((nq, 1), F32), jnp.zeros((nq, kvw), F32))
    carry = lax.fori_loop(0, past // tk, past_tile, carry)

    pad = LANE - cn_ref.shape[0]
    cb_new = jnp.concatenate([cn_ref[...], jnp.zeros((pad, kvw), F32)], axis=0).astype(BF16)
    kr_new = jnp.concatenate([krn_ref[...], jnp.zeros((pad, krn_ref.shape[1]), F32)], axis=0)
    s_new = _key_tile_scores(lhs_main, lhs_rope, cb_new, kr_new, cs_ref[pl.ds(past, LANE), :],
                             heads, width, scale)
    key_j = lax.broadcasted_iota(jnp.int32, (nq, LANE), 1)
    q_i = lax.broadcasted_iota(jnp.int32, (nq, LANE), 0) // heads
    s_new = jnp.where((key_j < lq) & (key_j <= q_i), s_new, -jnp.inf)
    _, l_fin, acc = update(carry, s_new, cb_new)

    lat = (acc / l_fin).astype(BF16)
    res = _dot(lat, wuv_ref[...]).reshape(lq, heads, heads * LANE)
    row_h = lax.broadcasted_iota(jnp.int32, (heads, heads * LANE), 0)
    col_h = lax.broadcasted_iota(jnp.int32, (heads, heads * LANE), 1) // LANE
    y_ref[...] = jnp.sum(jnp.where((row_h == col_h)[None], res, 0.0), axis=1)


def _sample_attention(page_table, wt, qlat, qg, c_new, kr_new, cs, wuv, cache_kv, cache_kr, layer,
                      heads, width, lq, tk):
    n_seq, n_pages = page_table.shape
    page = cache_kv.shape[2]
    kvw = cache_kv.shape[3]
    krw = cache_kr.shape[3]
    past = n_pages * page
    nq = lq * heads
    rows_new = c_new.shape[1]
    grid_spec = pltpu.PrefetchScalarGridSpec(
        num_scalar_prefetch=1,
        grid=(n_seq,),
        in_specs=[
            pl.BlockSpec((None,) + wt.shape[1:], lambda s, pt: (layer, 0, 0)),
            pl.BlockSpec((None, nq, kvw), lambda s, pt: (s, 0, 0)),
            pl.BlockSpec((None, nq, 2 * LANE), lambda s, pt: (s, 0, 0)),
            pl.BlockSpec((None, rows_new, kvw), lambda s, pt: (s, 0, 0)),
            pl.BlockSpec((None, rows_new, krw), lambda s, pt: (s, 0, 0)),
            pl.BlockSpec(cs.shape, lambda s, pt: (0, 0)),
            pl.BlockSpec((None,) + wuv.shape[1:], lambda s, pt: (layer, 0, 0)),
            pl.BlockSpec(memory_space=pl.ANY),
            pl.BlockSpec(memory_space=pl.ANY),
        ],
        out_specs=pl.BlockSpec((None, lq, heads * LANE), lambda s, pt: (s, 0, 0)),
        scratch_shapes=[pltpu.VMEM((2, past, kvw), F32), pltpu.VMEM((2, past, krw), F32),
                        pltpu.SemaphoreType.DMA((2, 2))],
    )
    return pl.pallas_call(
        functools.partial(_sattn_body, layer=layer, heads=heads, width=width, lq=lq, n_pages=n_pages,
                          page=page, tk=tk),
        grid_spec=grid_spec,
        out_shape=jax.ShapeDtypeStruct((n_seq, lq, heads * LANE), F32),
        compiler_params=_cparams("arbitrary"),
        name="sample_attention",
    )(page_table, wt, qlat, qg, c_new, kr_new, cs, wuv, cache_kv, cache_kr)


def _ln_swish(yc, g, b):
    mu = jnp.mean(yc, axis=-1, keepdims=True)
    xc = yc - mu
    y = xc * lax.rsqrt(jnp.mean(xc * xc, axis=-1, keepdims=True) + LN_EPS) * g + b
    return _silu(y)


def _conv_prompt_body(a_ref, gate_ref, w_ref, b_ref, lg_ref, lb_ref, act_ref, new_ref, xp_ref, slab_ref, *,
                      taps, chunk):
    seq = a_ref.shape[0]
    front = xp_ref.shape[0] - seq
    shift = front - (taps - 1)
    xp_ref[0:front, :] = jnp.zeros((front, xp_ref.shape[1]), F32)
    xp_ref[front:, :] = a_ref[...] * _sigmoid(gate_ref[...])
    new_ref[...] = xp_ref[seq + shift:seq + front, :]

    def body(i, carry):
        r0 = pl.multiple_of(i * chunk, chunk)
        slab_ref[...] = xp_ref[pl.ds(r0, chunk + front), :]
        acc = jnp.zeros((chunk, xp_ref.shape[1]), F32) + b_ref[...]
        for j in range(taps):
            acc = acc + slab_ref[shift + j:shift + j + chunk, :] * w_ref[j:j + 1, :]
        act_ref[pl.ds(r0, chunk), :] = _ln_swish(acc, lg_ref[...], lb_ref[...]).astype(BF16)
        return carry

    lax.fori_loop(0, seq // chunk, body, 0)


def _conv_prompt(z, cols, wts, layer, batch, seq, taps):
    ch = cols["glu_a"][1]
    a_off = cols["glu_a"][0] // ch
    g_off = cols["glu_g"][0] // ch
    front = -(-(taps - 1) // SUBLANE) * SUBLANE

    def wspec(a):
        return pl.BlockSpec((None,) + a.shape[1:], lambda b: (layer,) + (0,) * (a.ndim - 1))

    chunk = _tile(seq, 64)
    return pl.pallas_call(
        functools.partial(_conv_prompt_body, taps=taps, chunk=chunk),
        grid=(batch,),
        in_specs=[pl.BlockSpec((seq, ch), lambda b: (b, a_off)), pl.BlockSpec((seq, ch), lambda b: (b, g_off)),
                  wspec(wts["conv_w"]), wspec(wts["conv_b"]), wspec(wts["ln_g"]), wspec(wts["ln_b"])],
        out_specs=[pl.BlockSpec((seq, ch), lambda b: (b, 0)),
                   pl.BlockSpec((None, taps - 1, ch), lambda b: (b, 0, 0))],
        out_shape=[jax.ShapeDtypeStruct((batch * seq, ch), BF16),
                   jax.ShapeDtypeStruct((batch, taps - 1, ch), F32)],
        scratch_shapes=[pltpu.VMEM((front + seq, ch), F32), pltpu.VMEM((front + chunk, ch), F32)],
        compiler_params=_cparams("parallel"),
        name="conv_prompt",
    )(z, z, wts["conv_w"], wts["conv_b"], wts["ln_g"], wts["ln_b"])


def _conv_sample_body(a_ref, gate_ref, st_ref, w_ref, b_ref, lg_ref, lb_ref, act_ref, new_ref, xp_ref,
                      u_ref, yc_ref, *, taps, lq, nseq):
    hist = taps - 1
    u_ref[...] = a_ref[...] * _sigmoid(gate_ref[...])
    w = w_ref[...]
    for s in range(nseq):
        xp_ref[0:hist, :] = st_ref[s]
        xp_ref[hist:hist + lq, :] = u_ref[s * lq:(s + 1) * lq, :]
        new_ref[s] = xp_ref[lq:lq + hist, :]
        for t in range(lq):
            yc_ref[s * lq + t:s * lq + t + 1, :] = jnp.sum(xp_ref[t:t + taps, :] * w, axis=0, keepdims=True)
    act_ref[...] = _ln_swish(yc_ref[...] + b_ref[...], lg_ref[...], lb_ref[...]).astype(BF16)


def _conv_sample(z, state, cols, wts, layer, lq, taps, nseq):
    ch = cols["glu_a"][1]
    a_off = cols["glu_a"][0] // ch
    g_off = cols["glu_g"][0] // ch
    n = state.shape[1]
    rows = nseq * lq

    def wspec(a):
        return pl.BlockSpec((None,) + a.shape[1:], lambda b: (layer,) + (0,) * (a.ndim - 1))

    return pl.pallas_call(
        functools.partial(_conv_sample_body, taps=taps, lq=lq, nseq=nseq),
        grid=(n // nseq,),
        in_specs=[pl.BlockSpec((rows, ch), lambda b: (b, a_off)), pl.BlockSpec((rows, ch), lambda b: (b, g_off)),
                  pl.BlockSpec((None, nseq, taps - 1, ch), lambda b: (layer, b, 0, 0)),
                  wspec(wts["conv_w"]), wspec(wts["conv_b"]), wspec(wts["ln_g"]), wspec(wts["ln_b"])],
        out_specs=[pl.BlockSpec((rows, ch), lambda b: (b, 0)),
                   pl.BlockSpec((nseq, taps - 1, ch), lambda b: (b, 0, 0))],
        out_shape=[jax.ShapeDtypeStruct((n * lq, ch), BF16),
                   jax.ShapeDtypeStruct((n, taps - 1, ch), F32)],
        scratch_shapes=[pltpu.VMEM((-(-(taps - 1 + lq) // SUBLANE) * SUBLANE, ch), F32),
                        pltpu.VMEM((rows, ch), F32), pltpu.VMEM((rows, ch), F32)],
        compiler_params=_cparams("parallel"),
        name="conv_sample",
    )(z, z, state, wts["conv_w"], wts["conv_b"], wts["ln_g"], wts["ln_b"])


def _log_sigmoid(x):
    return jnp.minimum(x, 0.0) - jnp.log1p(jnp.exp(-jnp.abs(x)))


def _split3(x):
    h1 = x.astype(BF16)
    r1 = x - h1.astype(F32)
    h2 = r1.astype(BF16)
    h3 = (r1 - h2.astype(F32)).astype(BF16)
    return h1, h2, h3


def _gla_body(gq_ref, gk_ref, gv_ref, ga_ref, gr_ref, s0_ref, wa_ref, ba_ref, gn_ref, y_ref, sout_ref,
              st_ref, oi_ref, *, heads, dk, dv, valid, carried):
    T = GLA_TILE
    R = GLA_BLOCK
    nb = T // R
    pairs = heads // 2
    hk = heads * dk
    ti = pl.program_id(1)

    if carried:
        @pl.when(ti == 0)
        def _():
            st_ref[...] = jnp.zeros(st_ref.shape, F32)

    row = lax.broadcasted_iota(jnp.int32, (T, T), 0)
    col = lax.broadcasted_iota(jnp.int32, (T, T), 1)
    same = (row // R) == (col // R)
    tri = jnp.where(same & (col <= row), 1.0, 0.0).astype(BF16)
    ones_blk = jnp.where(same, 1.0, 0.0).astype(BF16)

    la = _log_sigmoid(_dot(ga_ref[...].astype(BF16), wa_ref[...]) + ba_ref[...]) * (1.0 / GLA_TAU)
    if valid < R:
        rr = lax.broadcasted_iota(jnp.int32, (T, hk), 0) % R
        la = jnp.where(rr < valid, la, 0.0)
    parts = jnp.concatenate(_split3(la), axis=-1)
    cums = _dot(jnp.concatenate([tri, ones_blk], axis=0), parts)
    bc = cums[:T, :hk] + cums[:T, hk:2 * hk] + cums[:T, 2 * hk:]
    bl = cums[T:, :hk] + cums[T:, hk:2 * hk] + cums[T:, 2 * hk:]
    q = gq_ref[...] * (dk ** -0.5)
    k = gk_ref[...]
    v = gv_ref[...]
    qe = q * jnp.exp(bc)
    kd = k * jnp.exp(bl - bc)
    dec = jnp.exp(bl)

    hd_row = lax.broadcasted_iota(jnp.int32, (hk, heads * dv), 0) // dk
    hd_col = lax.broadcasted_iota(jnp.int32, (hk, heads * dv), 1) // dv
    expand = jnp.where(hd_row == hd_col, 1.0, 0.0).astype(BF16)
    jrow = lax.broadcasted_iota(jnp.int32, (R, hk), 0)
    for t in range(nb):
        rs = slice(t * R, (t + 1) * R)
        b, qb, kb, vb = bc[rs], q[rs], k[rs], v[rs]
        p = []
        for i in range(R):
            e = jnp.exp(jnp.where(jrow <= i, b[i:i + 1, :] - b, -jnp.inf))
            p.append((e * qb[i:i + 1, :] * kb).astype(BF16))
        a = _dot(jnp.concatenate(p, axis=0), expand)
        for i in range(R):
            oi_ref[t * R + i:t * R + i + 1, :] = jnp.sum(a[i * R:(i + 1) * R] * vb, axis=0, keepdims=True)
    o = oi_ref[...]

    lane = lax.broadcasted_iota(jnp.int32, (T, 2 * dk), 1)
    blk = lax.broadcasted_iota(jnp.int32, (T, 2 * dk), 0) // R
    lane_blk = lax.broadcasted_iota(jnp.int32, (R, 2 * dk), 1)
    o_cols = []
    for pr in range(pairs):
        ls = slice(pr * 2 * dk, (pr + 1) * 2 * dk)
        kd_p, qe_p, dec_p = kd[:, ls], qe[:, ls], dec[:, ls]
        kv = None
        for hh in range(2):
            h = 2 * pr + hh
            kd_h = jnp.where((lane // dk) == hh, kd_p, 0.0)
            stack = jnp.concatenate([jnp.where(blk == t, kd_h, 0.0) for t in range(nb)], axis=-1).astype(BF16)
            vt = v[:, h * dv:(h + 1) * dv].T.astype(BF16)
            part = _dot(vt, stack)
            kv = part if kv is None else kv + part
        o_pair = [[], []]
        for t in range(nb):
            rs = slice(t * R, (t + 1) * R)
            if carried:
                st = st_ref[pr]
            else:
                s0 = s0_ref[t]
                st = jnp.concatenate([s0[2 * pr], s0[2 * pr + 1]], axis=0).T
            stb = st.astype(BF16)
            for hh in range(2):
                qm = jnp.where((lane_blk // dk) == hh, qe_p[rs], 0.0).astype(BF16)
                o_pair[hh].append(_dot_nt(qm, stb))
            st = st * dec_p[t * R:t * R + 1, :] + kv[:, t * 2 * dk:(t + 1) * 2 * dk]
            if carried:
                st_ref[pr] = st
            else:
                stt = st.T
                sout_ref[t, 2 * pr] = stt[:dk]
                sout_ref[t, 2 * pr + 1] = stt[dk:]
        for hh in range(2):
            o_cols.append(jnp.concatenate(o_pair[hh], axis=0))
    o = o + jnp.concatenate(o_cols, axis=-1)

    gr = gr_ref[...]
    for h in range(heads):
        cs = slice(h * dv, (h + 1) * dv)
        y_ref[:, cs] = (_rms(o[:, cs], gn_ref[...]) * _silu(gr[:, cs])).astype(BF16)

    if carried:
        @pl.when(ti == pl.num_programs(1) - 1)
        def _():
            for pr in range(pairs):
                stt = st_ref[pr].T
                sout_ref[2 * pr] = stt[:dk]
                sout_ref[2 * pr + 1] = stt[dk:]


def _gla(zg, cols, s0, wts, layer, batch, seq, heads, dk, dv, valid, carried):
    T = GLA_TILE
    m = zg.shape[0]
    hk, hv = heads * dk, heads * dv
    if carried:
        grid = (batch, seq // T)
        rowblk = lambda b, t: b * (seq // T) + t
        s_spec = pl.BlockSpec((None, heads, dk, dv), lambda b, t: (b, 0, 0, 0))
        s0_arr = jnp.zeros((1, heads, dk, dv), F32)
        s0_spec = pl.BlockSpec((1, heads, dk, dv), lambda b, t: (0, 0, 0, 0))
        n_state = batch
    else:
        nb = T // GLA_BLOCK
        grid = (1, m // T)
        rowblk = lambda b, t: t
        s_spec = pl.BlockSpec((nb, heads, dk, dv), lambda b, t: (t, 0, 0, 0))
        s0_arr = s0
        s0_spec = pl.BlockSpec((None, nb, heads, dk, dv), lambda b, t: (layer, t, 0, 0, 0))
        n_state = m // GLA_BLOCK

    def zspec(name):
        off, w = cols[name]
        return pl.BlockSpec((T, w), lambda b, t: (rowblk(b, t), off // w))

    def wspec(a):
        return pl.BlockSpec((None,) + a.shape[1:], lambda b, t: (layer,) + (0,) * (a.ndim - 1))

    return pl.pallas_call(
        functools.partial(_gla_body, heads=heads, dk=dk, dv=dv, valid=valid, carried=carried),
        grid=grid,
        in_specs=[zspec("gq"), zspec("gk"), zspec("gv"), zspec("ga"), zspec("gr"), s0_spec,
                  wspec(wts["gla_wa"]), wspec(wts["gla_ba"]), wspec(wts["gla_norm"])],
        out_specs=[pl.BlockSpec((T, hv), lambda b, t: (rowblk(b, t), 0)), s_spec],
        out_shape=[jax.ShapeDtypeStruct((m, hv), BF16), jax.ShapeDtypeStruct((n_state, heads, dk, dv), F32)],
        scratch_shapes=[pltpu.VMEM((heads // 2, dv, 2 * dk), F32), pltpu.VMEM((T, hv), F32)],
        compiler_params=_cparams("parallel", "arbitrary"),
        name="gla_carried" if carried else "gla_blocks",
    )(zg, zg, zg, zg, zg, s0_arr, wts["gla_wa"], wts["gla_ba"], wts["gla_norm"])


def _merge_body(ya_ref, yb_ref, yc_ref, g0_ref, g1_ref, g2_ref, wpa_ref, wpb_ref, wpc_ref, m_ref):
    m = _sigmoid(g0_ref[...]) * _dot(ya_ref[...], wpa_ref[...])
    m = m + _sigmoid(g1_ref[...]) * _dot(yb_ref[...], wpb_ref[...])
    m = m + _sigmoid(g2_ref[...]) * _dot(yc_ref[...], wpc_ref[...])
    m_ref[...] = m.astype(BF16)


def _merge(ya, yb, yc, z, cols, wts, layer, tm):
    m = ya.shape[0]
    d = wts["w_pa"].shape[-1]
    mg_off = cols["mg"][0] // d

    def wspec(a):
        return pl.BlockSpec((None,) + a.shape[1:], lambda i: (layer,) + (0,) * (a.ndim - 1))

    def gspec(idx):
        return pl.BlockSpec((tm, d), lambda i: (i, mg_off + idx))

    def yspec(a):
        return pl.BlockSpec((tm, a.shape[1]), lambda i: (i, 0))

    return pl.pallas_call(
        _merge_body,
        grid=(m // tm,),
        in_specs=[yspec(ya), yspec(yb), yspec(yc), gspec(0), gspec(1), gspec(2),
                  wspec(wts["w_pa"]), wspec(wts["w_pb"]), wspec(wts["w_pc"])],
        out_specs=pl.BlockSpec((tm, d), lambda i: (i, 0)),
        out_shape=jax.ShapeDtypeStruct((m, d), BF16),
        compiler_params=_cparams("parallel"),
        name="merge",
    )(ya, yb, yc, z, z, z, wts["w_pa"], wts["w_pb"], wts["w_pc"])


def _out_proj_body(x_ref, m_ref, w_ref, o_ref):
    o_ref[...] = x_ref[...] + _dot(m_ref[...], w_ref[...])


def _out_proj(x, mm, w, layer, tm, tn):
    m, d = x.shape
    return pl.pallas_call(
        _out_proj_body,
        grid=(m // tm, d // tn),
        in_specs=[pl.BlockSpec((tm, tn), lambda i, j: (i, j)),
                  pl.BlockSpec((tm, d), lambda i, j: (i, 0)),
                  pl.BlockSpec((None, d, tn), lambda i, j: (layer, 0, j))],
        out_specs=pl.BlockSpec((tm, tn), lambda i, j: (i, j)),
        out_shape=jax.ShapeDtypeStruct((m, d), F32),
        compiler_params=_cparams("parallel", "parallel"),
        name="out_proj",
    )(x, mm, w)


def _ffn_body(x_ref, g_ref, wg_ref, wu_ref, wo_ref, o_ref, h_ref, acc_ref):
    f = pl.program_id(1)

    @pl.when(f == 0)
    def _():
        h_ref[...] = _rms(x_ref[...], g_ref[...]).astype(BF16)
        acc_ref[...] = jnp.zeros(acc_ref.shape, F32)

    h = h_ref[...]
    act = (_silu(_dot(h, wg_ref[...])) * _dot(h, wu_ref[...])).astype(BF16)
    acc_ref[...] += _dot(act, wo_ref[...])

    @pl.when(f == pl.num_programs(1) - 1)
    def _():
        o_ref[...] = x_ref[...] + acc_ref[...]


def _ffn(x, g, w_in, w_out, layer, tm, tf):
    m, d = x.shape
    dff = w_out.shape[1]
    nf = dff // tf
    return pl.pallas_call(
        _ffn_body,
        grid=(m // tm, nf),
        in_specs=[pl.BlockSpec((tm, d), lambda i, f: (i, 0)),
                  pl.BlockSpec((None, 1, d), lambda i, f: (layer, 0, 0)),
                  pl.BlockSpec((None, d, tf), lambda i, f: (layer, 0, f)),
                  pl.BlockSpec((None, d, tf), lambda i, f: (layer, 0, nf + f)),
                  pl.BlockSpec((None, tf, d), lambda i, f: (layer, f, 0))],
        out_specs=pl.BlockSpec((tm, d), lambda i, f: (i, 0)),
        out_shape=jax.ShapeDtypeStruct((m, d), F32),
        scratch_shapes=[pltpu.VMEM((tm, d), BF16), pltpu.VMEM((tm, d), F32)],
        compiler_params=_cparams("parallel", "arbitrary"),
        name="ffn",
    )(x, g, w_in, w_in, w_out)


def _rope_angles(pos):
    half = 32
    inv = jnp.exp(-math.log(ROPE_BASE) * jnp.arange(half, dtype=F32) / half)
    ang = pos.astype(F32)[:, None] * inv[None, :]
    return jnp.cos(ang), jnp.sin(ang)


def _rope_tables(pos):
    cos, sin = _rope_angles(pos)
    z = jnp.zeros((pos.shape[0], LANE // 2), F32)
    return jnp.concatenate([cos, cos, z], -1), jnp.concatenate([-sin, sin, z], -1)


def _pad_cols(a, width):
    return jnp.pad(a, [(0, 0)] * (a.ndim - 1) + [(0, width - a.shape[-1])])


def kernel(x_prompt, x_sample, cache_kv, cache_krope, state_conv, state_gla, page_table, attn_norm, w_in, cq_norm, w_uq, q_norm, kv_norm, w_uk, k_norm, w_uv, w_pa, conv_w, conv_b, conv_ln_g, conv_ln_b, w_pb, gla_wa, gla_ba, gla_norm, w_pc, w_o, ffn_norm, w_ffn_in, w_ffn_out):
    depth, d_model, _ = w_in.shape
    batch, seq, _ = x_prompt.shape
    n_seq, lq, _ = x_sample.shape
    q_lora = cq_norm.shape[1]
    kv_lora = kv_norm.shape[1]
    heads, qk_head = w_uq.shape[2], w_uq.shape[3]
    qk_nope = w_uk.shape[3]
    qk_rope = qk_head - qk_nope
    conv_ch = conv_b.shape[1]
    taps = conv_w.shape[1]
    g_heads, g_dk, g_dv = state_gla.shape[2], state_gla.shape[3], state_gla.shape[4]
    g_rank = gla_wa.shape[1]
    n_pages, page = page_table.shape[1], cache_kv.shape[2]
    past = n_pages * page
    assert (qk_nope, qk_rope, kv_lora) == (LANE, LANE // 2, 2 * LANE) and g_heads % 2 == 0 and 2 * g_dk == LANE

    ref_sizes = dict(cq=q_lora, ckv=kv_lora, kr=qk_rope, glu=2 * conv_ch, gq=g_heads * g_dk, gk=g_heads * g_dk,
                     gv=g_heads * g_dv, ga=g_rank, gr=g_heads * g_dv, mg=3 * d_model)
    ref_off, off = {}, 0
    for name, size in ref_sizes.items():
        ref_off[name] = off
        off += size
    pieces = [("mg", ref_off["mg"], 3 * d_model, 3 * d_model),
              ("glu_a", ref_off["glu"], conv_ch, conv_ch), ("glu_g", ref_off["glu"] + conv_ch, conv_ch, conv_ch),
              ("cq", ref_off["cq"], q_lora, q_lora), ("gv", ref_off["gv"], g_heads * g_dv, g_heads * g_dv),
              ("gr", ref_off["gr"], g_heads * g_dv, g_heads * g_dv), ("ckv", ref_off["ckv"], kv_lora, kv_lora),
              ("gq", ref_off["gq"], g_heads * g_dk, g_heads * g_dk), ("gk", ref_off["gk"], g_heads * g_dk, g_heads * g_dk),
              ("kr", ref_off["kr"], qk_rope, LANE), ("ga", ref_off["ga"], g_rank, LANE)]
    cols, off, packed = {}, 0, []
    for name, src, size, width in pieces:
        cols[name] = (off, width)
        packed.append(_pad_cols(w_in[:, :, src:src + size], width))
        off += width
    cols["mg"] = (0, d_model)
    w_in_p = jnp.concatenate(packed, axis=-1).astype(BF16)
    n_in = w_in_p.shape[-1]

    wuq_p = jnp.concatenate([w_uq[..., :qk_nope], _pad_cols(w_uq[..., qk_nope:], LANE)], axis=-1)
    wuq_p = wuq_p.reshape(depth, q_lora, heads * HEAD_PAD).astype(BF16)
    pad_gain = lambda g: jnp.concatenate([g[:, :qk_nope], _pad_cols(g[:, qk_nope:], LANE)], -1)[:, None, :]
    half = qk_rope // 2
    g_r1, g_r2 = k_norm[:, qk_nope:qk_nope + half], k_norm[:, qk_nope + half:]
    zeros_h = jnp.zeros_like(g_r1)
    attn_w = dict(
        g_cq=cq_norm[:, None, :], w_uq=wuq_p, g_q=pad_gain(q_norm), g_kv=kv_norm[:, None, :],
        w_uk=w_uk.reshape(depth, kv_lora, heads * qk_nope).astype(BF16), g_k=pad_gain(k_norm),
        w_uk3=jnp.transpose(w_uk, (0, 2, 1, 3)).astype(BF16),
        g_t1=jnp.concatenate([g_r1, g_r2, zeros_h, zeros_h], -1)[:, None, :],
        g_t2=jnp.concatenate([zeros_h, zeros_h, g_r1, -g_r2], -1)[:, None, :],
    )
    w_ukT = jnp.transpose(w_uk.reshape(depth, kv_lora, heads * qk_nope), (0, 2, 1)).astype(BF16)
    w_uv_heads = jnp.transpose(w_uv, (0, 2, 1, 3)).astype(BF16)
    w_uv_flat = w_uv.reshape(depth, kv_lora, heads * w_uv.shape[3]).astype(BF16)
    conv_wts = dict(conv_w=conv_w, conv_b=conv_b[:, None, :], ln_g=conv_ln_g[:, None, :], ln_b=conv_ln_b[:, None, :])
    gla_wts = dict(gla_wa=jnp.pad(gla_wa, ((0, 0), (0, LANE - g_rank), (0, 0))).astype(BF16),
                   gla_ba=gla_ba[:, None, :], gla_norm=gla_norm[:, None, :])
    merge_wts = dict(w_pa=w_pa.astype(BF16), w_pb=w_pb.astype(BF16), w_pc=w_pc.astype(BF16))
    w_o_b = w_o.astype(BF16)
    w_ffn_in_b = w_ffn_in.astype(BF16)
    w_ffn_out_b = w_ffn_out.astype(BF16)
    g_attn = attn_norm[:, None, :]
    g_ffn = ffn_norm[:, None, :]

    cos_p, sin_p = _rope_tables(jnp.arange(seq, dtype=jnp.int32))
    cos_s, sin_s = _rope_tables(past + jnp.arange(lq, dtype=jnp.int32))
    cos_s, sin_s = jnp.tile(cos_s, (n_seq, 1)), jnp.tile(sin_s, (n_seq, 1))
    kc, ks = _rope_angles(jnp.arange(past + LANE, dtype=jnp.int32))
    key_cs = jnp.concatenate([kc, kc, ks, ks], axis=-1)

    mp, ms = batch * seq, n_seq * lq
    tm_p = _tile(mp, 1024)
    tn_in = _tile(n_in, 512)
    tf = _tile(w_ffn_out.shape[1], 512)
    gla_cols = {k: cols[k] for k in ("gq", "gk", "gv", "ga", "gr")}
    gla_lo = min(v[0] for v in gla_cols.values())

    xp = x_prompt.reshape(mp, d_model)
    xs = x_sample.reshape(ms, d_model)
    outs = {k: [] for k in ("ckv_p", "kr_p", "conv_p", "gla_p", "ckv_s", "kr_s", "conv_s", "gla_s")}
    kr_off = cols["kr"][0]
    for l in range(depth):
        z = _norm_matmul(xp, g_attn, w_in_p, l, tm_p, tn_in)
        q, k, ckv, cb = _qk_prompt(z, cols, cos_p, sin_p, attn_w, l, _tile(seq, 256), heads, qk_head)
        ya = _prompt_attention(q, k, cb, w_uv_heads, l, batch, seq, heads, qk_head, _tile(seq, 512), _tile(seq, 512))
        yb, conv_new = _conv_prompt(z, cols, conv_wts, l, batch, seq, taps)
        yc, gla_new = _gla(z, gla_cols, None, gla_wts, l, batch, seq, g_heads, g_dk, g_dv, GLA_BLOCK, True)
        mm = _merge(ya, yb, yc, z, cols, merge_wts, l, _tile(mp, 256))
        x1 = _out_proj(xp, mm, w_o_b, l, tm_p, _tile(d_model, 512))
        xp = _ffn(x1, g_ffn, w_ffn_in_b, w_ffn_out_b, l, _tile(mp, 512), tf)
        outs["ckv_p"].append(ckv.reshape(batch, seq, kv_lora))
        outs["kr_p"].append(z[:, kr_off:kr_off + qk_rope].reshape(batch, seq, qk_rope))
        outs["conv_p"].append(conv_new)
        outs["gla_p"].append(gla_new)

        z = _norm_matmul(xs, g_attn, w_in_p, l, ms, tn_in)
        qlat, qg, ckv = _q_sample(z, cols, cos_s, sin_s, attn_w, l, heads, qk_head)
        kr_new = z[:, kr_off:kr_off + qk_rope]
        rows_new = -(-lq // SUBLANE) * SUBLANE
        c_new = jnp.pad(ckv.reshape(n_seq, lq, kv_lora), ((0, 0), (0, rows_new - lq), (0, 0)))
        kr_pad = jnp.pad(kr_new.reshape(n_seq, lq, qk_rope), ((0, 0), (0, rows_new - lq), (0, 0)))
        ya = _sample_attention(page_table, w_ukT, qlat.reshape(n_seq, lq * heads, kv_lora),
                               qg.reshape(n_seq, lq * heads, 2 * LANE), c_new, kr_pad, key_cs, w_uv_flat,
                               cache_kv, cache_krope, l, heads, qk_head, lq, _tile(past, 512))
        ya = ya.reshape(ms, heads * LANE).astype(BF16)
        yb, conv_new = _conv_sample(z, state_conv, cols, conv_wts, l, lq, taps, SUBLANE)
        zg = z[:, gla_lo:]
        zg = jnp.pad(zg.reshape(n_seq, lq, -1), ((0, 0), (0, GLA_BLOCK - lq), (0, 0))).reshape(n_seq * GLA_BLOCK, -1)
        zg_cols = {k: (v[0] - gla_lo, v[1]) for k, v in gla_cols.items()}
        yc, gla_new = _gla(zg, zg_cols, state_gla, gla_wts, l, 1, GLA_BLOCK, g_heads, g_dk, g_dv, lq, False)
        yc = yc.reshape(n_seq, GLA_BLOCK, -1)[:, :lq].reshape(ms, -1)
        mm = _merge(ya, yb, yc, z, cols, merge_wts, l, _tile(ms, 256))
        x1 = _out_proj(xs, mm, w_o_b, l, ms, _tile(d_model, 512))
        xs = _ffn(x1, g_ffn, w_ffn_in_b, w_ffn_out_b, l, ms, tf)
        outs["ckv_s"].append(ckv.reshape(n_seq, lq, kv_lora))
        outs["kr_s"].append(kr_new.reshape(n_seq, lq, qk_rope))
        outs["conv_s"].append(conv_new)
        outs["gla_s"].append(gla_new)

    stack = lambda key: jnp.stack(outs[key])
    return (xp.reshape(batch, seq, d_model), xs.reshape(n_seq, lq, d_model),
            stack("ckv_p"), stack("kr_p"), stack("conv_p"), stack("gla_p"),
            stack("ckv_s"), stack("kr_s"), stack("conv_s"), stack("gla_s"))
```

```python
import functools
import math

import jax
import jax.numpy as jnp
from jax import lax
from jax.experimental import pallas as pl
from jax.experimental.pallas import tpu as pltpu

F32 = jnp.float32
BF16 = jnp.bfloat16

NORM_EPS = 1e-6
LN_EPS = 1e-5
ROPE_BASE = 10000.0
GLA_TAU = 16.0

LANE = 128
SUBLANE = 8
VMEM_LIMIT = 56 * 1024 * 1024
GLA_BLOCK = 16
GLA_TILE = 128
HEAD_PAD = 256


def _cparams(*sem):
    return pltpu.CompilerParams(dimension_semantics=sem, vmem_limit_bytes=VMEM_LIMIT)


def _tile(n, pref):
    if n <= pref:
        return n
    t = pref
    while n % t:
        t -= SUBLANE
    return t


def _rms(x, g, eps=NORM_EPS):
    return x * lax.rsqrt(jnp.mean(x * x, axis=-1, keepdims=True) + eps) * g


def _dot(a, b):
    return jnp.dot(a, b, preferred_element_type=F32)


def _dot_nt(a, b):
    return lax.dot_general(a, b, (((1,), (1,)), ((), ())), preferred_element_type=F32)


def _sigmoid(x):
    return 1.0 / (1.0 + jnp.exp(-x))


def _silu(x):
    return x * _sigmoid(x)


def _rope_lanes(x, cos_t, sin_t):
    return x * cos_t + (pltpu.roll(x, 96, 1) + pltpu.roll(x, 32, 1)) * sin_t


def _norm_matmul_body(x_ref, g_ref, w_ref, o_ref, h_ref):
    @pl.when(pl.program_id(1) == 0)
    def _():
        h_ref[...] = _rms(x_ref[...], g_ref[...]).astype(BF16)

    o_ref[...] = _dot(h_ref[...], w_ref[...])


def _norm_matmul(x, g, w, layer, tm, tn):
    m, k = x.shape
    n = w.shape[-1]
    return pl.pallas_call(
        _norm_matmul_body,
        grid=(m // tm, n // tn),
        in_specs=[
            pl.BlockSpec((tm, k), lambda i, j: (i, 0)),
            pl.BlockSpec((None, 1, k), lambda i, j: (layer, 0, 0)),
            pl.BlockSpec((None, k, tn), lambda i, j: (layer, 0, j)),
        ],
        out_specs=pl.BlockSpec((tm, tn), lambda i, j: (i, j)),
        out_shape=jax.ShapeDtypeStruct((m, n), F32),
        scratch_shapes=[pltpu.VMEM((tm, k), BF16)],
        compiler_params=_cparams("parallel", "arbitrary"),
        name="in_proj",
    )(x, g, w)


def _head_norm_q(qraw, h, gq_ref, cos_t, sin_t, width):
    lo = h * HEAD_PAD
    qn = qraw[:, lo:lo + LANE]
    qr = qraw[:, lo + LANE:lo + HEAD_PAD]
    ss = jnp.sum(qn * qn, axis=-1, keepdims=True) + jnp.sum(qr * qr, axis=-1, keepdims=True)
    rinv = lax.rsqrt(ss * (1.0 / width) + NORM_EPS)
    qn = qn * rinv * gq_ref[:, :LANE]
    qr = _rope_lanes(qr * rinv * gq_ref[:, LANE:], cos_t, sin_t)
    return qn, qr


def _qk_prompt_body(zcq_ref, zckv_ref, zkr_ref, cos_ref, sin_ref, gcq_ref, wuq_ref, gq_ref, gkv_ref,
                    wuk_ref, gk_ref, q_ref, k_ref, ckv_ref, cb_ref, *, heads, width):
    cos_t = cos_ref[...]
    sin_t = sin_ref[...]
    cq = _rms(zcq_ref[...], gcq_ref[...]).astype(BF16)
    qraw = _dot(cq, wuq_ref[...])
    c = _rms(zckv_ref[...], gkv_ref[...])
    ckv_ref[...] = c
    cb = c.astype(BF16)
    cb_ref[...] = cb
    kn = _dot(cb, wuk_ref[...])
    krp = zkr_ref[...]
    kr_ss = jnp.sum(krp * krp, axis=-1, keepdims=True)
    for h in range(heads):
        qn, qr = _head_norm_q(qraw, h, gq_ref, cos_t, sin_t, width)
        lo = h * HEAD_PAD
        q_ref[:, lo:lo + LANE] = qn.astype(BF16)
        q_ref[:, lo + LANE:lo + HEAD_PAD] = qr.astype(BF16)
        kh = kn[:, h * LANE:(h + 1) * LANE]
        rk = lax.rsqrt((jnp.sum(kh * kh, axis=-1, keepdims=True) + kr_ss) * (1.0 / width) + NORM_EPS)
        k_ref[:, lo:lo + LANE] = (kh * rk * gk_ref[:, :LANE]).astype(BF16)
        k_ref[:, lo + LANE:lo + HEAD_PAD] = _rope_lanes(krp * rk * gk_ref[:, LANE:], cos_t, sin_t).astype(BF16)


def _qk_prompt(z, cols, cos_t, sin_t, wts, layer, tm, heads, width):
    m = z.shape[0]
    n_tab = cos_t.shape[0] // tm
    hp = heads * HEAD_PAD

    def zspec(name):
        off, w = cols[name]
        return pl.BlockSpec((tm, w), lambda i: (i, off // w))

    def wspec(a):
        return pl.BlockSpec((None,) + a.shape[1:], lambda i: (layer,) + (0,) * (a.ndim - 1))

    tab = pl.BlockSpec((tm, LANE), lambda i: (i % n_tab, 0))
    return pl.pallas_call(
        functools.partial(_qk_prompt_body, heads=heads, width=width),
        grid=(m // tm,),
        in_specs=[zspec("cq"), zspec("ckv"), zspec("kr"), tab, tab,
                  wspec(wts["g_cq"]), wspec(wts["w_uq"]), wspec(wts["g_q"]), wspec(wts["g_kv"]),
                  wspec(wts["w_uk"]), wspec(wts["g_k"])],
        out_specs=[pl.BlockSpec((tm, hp), lambda i: (i, 0)), pl.BlockSpec((tm, hp), lambda i: (i, 0)),
                   pl.BlockSpec((tm, cols["ckv"][1]), lambda i: (i, 0)),
                   pl.BlockSpec((tm, cols["ckv"][1]), lambda i: (i, 0))],
        out_shape=[jax.ShapeDtypeStruct((m, hp), BF16), jax.ShapeDtypeStruct((m, hp), BF16),
                   jax.ShapeDtypeStruct((m, cols["ckv"][1]), F32),
                   jax.ShapeDtypeStruct((m, cols["ckv"][1]), BF16)],
        compiler_params=_cparams("parallel"),
        name="qk_prompt",
    )(z, z, z, cos_t, sin_t, wts["g_cq"], wts["w_uq"], wts["g_q"], wts["g_kv"], wts["w_uk"], wts["g_k"])


def _q_sample_body(zcq_ref, zckv_ref, cos_ref, sin_ref, gcq_ref, wuq_ref, gq_ref, gkv_ref, wuk3_ref,
                   gk_ref, gt1_ref, gt2_ref, qlat_ref, qg_ref, ckv_ref, *, heads, width):
    cos_t = cos_ref[...]
    sin_t = sin_ref[...]
    cq = _rms(zcq_ref[...], gcq_ref[...]).astype(BF16)
    qraw = _dot(cq, wuq_ref[...])
    ckv_ref[...] = _rms(zckv_ref[...], gkv_ref[...])
    zero = jnp.zeros((qraw.shape[0], LANE), BF16)
    for h in range(heads):
        qn, qr = _head_norm_q(qraw, h, gq_ref, cos_t, sin_t, width)
        lo = h * HEAD_PAD
        qlat_ref[:, lo:lo + HEAD_PAD] = _dot_nt((qn * gk_ref[:, :LANE]).astype(BF16), wuk3_ref[h]).astype(BF16)
        feat = qr * gt1_ref[...] + (pltpu.roll(qr, 32, 1) + pltpu.roll(qr, 96, 1)) * gt2_ref[...]
        qg_ref[:, lo:lo + LANE] = feat.astype(BF16)
        qg_ref[:, lo + LANE:lo + HEAD_PAD] = zero


def _q_sample(z, cols, cos_t, sin_t, wts, layer, heads, width):
    m = z.shape[0]
    tm = m
    hp = heads * HEAD_PAD

    def zspec(name):
        off, w = cols[name]
        return pl.BlockSpec((tm, w), lambda i: (i, off // w))

    def wspec(a):
        return pl.BlockSpec((None,) + a.shape[1:], lambda i: (layer,) + (0,) * (a.ndim - 1))

    tab = pl.BlockSpec((tm, LANE), lambda i: (i, 0))
    kvw = cols["ckv"][1]
    return pl.pallas_call(
        functools.partial(_q_sample_body, heads=heads, width=width),
        grid=(m // tm,),
        in_specs=[zspec("cq"), zspec("ckv"), tab, tab,
                  wspec(wts["g_cq"]), wspec(wts["w_uq"]), wspec(wts["g_q"]), wspec(wts["g_kv"]),
                  wspec(wts["w_uk3"]), wspec(wts["g_k"]), wspec(wts["g_t1"]), wspec(wts["g_t2"])],
        out_specs=[pl.BlockSpec((tm, hp), lambda i: (i, 0)), pl.BlockSpec((tm, hp), lambda i: (i, 0)),
                   pl.BlockSpec((tm, kvw), lambda i: (i, 0))],
        out_shape=[jax.ShapeDtypeStruct((m, hp), BF16), jax.ShapeDtypeStruct((m, hp), BF16),
                   jax.ShapeDtypeStruct((m, kvw), F32)],
        compiler_params=_cparams("parallel"),
        name="q_sample",
    )(z, z, cos_t, sin_t, wts["g_cq"], wts["w_uq"], wts["g_q"], wts["g_kv"], wts["w_uk3"], wts["g_k"],
      wts["g_t1"], wts["g_t2"])


def _pattn_body(q_ref, k_ref, c_ref, wuv_ref, y_ref, m_ref, l_ref, acc_ref, *, heads, tq, tk, scale):
    qi = pl.program_id(1)
    ki = pl.program_id(2)

    @pl.when(ki == 0)
    def _():
        m_ref[...] = jnp.full(m_ref.shape, -jnp.inf, F32)
        l_ref[...] = jnp.zeros(l_ref.shape, F32)
        acc_ref[...] = jnp.zeros(acc_ref.shape, F32)

    @pl.when(ki * tk <= qi * tq + tq - 1)
    def _():
        qpos = qi * tq + lax.broadcasted_iota(jnp.int32, (tq, tk), 0)
        kpos = ki * tk + lax.broadcasted_iota(jnp.int32, (tq, tk), 1)
        visible = kpos <= qpos
        c = c_ref[...]
        for h in range(heads):
            lo = h * HEAD_PAD
            s = _dot_nt(q_ref[:, lo:lo + HEAD_PAD], k_ref[:, lo:lo + HEAD_PAD]) * scale
            s = jnp.where(visible, s, -jnp.inf)
            m_prev = m_ref[h]
            m_new = jnp.maximum(m_prev, jnp.max(s, axis=-1, keepdims=True))
            alpha = jnp.exp(m_prev - m_new)
            p = jnp.exp(s - m_new)
            l_ref[h] = alpha * l_ref[h] + jnp.sum(p, axis=-1, keepdims=True)
            acc_ref[h] = alpha * acc_ref[h] + _dot(p.astype(BF16), c)
            m_ref[h] = m_new

    @pl.when(ki == pl.num_programs(2) - 1)
    def _():
        for h in range(heads):
            lat = (acc_ref[h] / l_ref[h]).astype(BF16)
            y_ref[:, h * LANE:(h + 1) * LANE] = _dot(lat, wuv_ref[h]).astype(BF16)


def _prompt_attention(q, k, cb, w_uv, layer, batch, seq, heads, width, tq, tk):
    m = q.shape[0]
    nq, nk = seq // tq, seq // tk
    hp = heads * HEAD_PAD
    kvw = cb.shape[1]

    def kv_idx(b, i, j):
        return (b * nk + jnp.minimum(j, (i * tq + tq - 1) // tk), 0)

    return pl.pallas_call(
        functools.partial(_pattn_body, heads=heads, tq=tq, tk=tk, scale=width ** -0.5),
        grid=(batch, nq, nk),
        in_specs=[pl.BlockSpec((tq, hp), lambda b, i, j: (b * nq + i, 0)),
                  pl.BlockSpec((tk, hp), kv_idx),
                  pl.BlockSpec((tk, kvw), kv_idx),
                  pl.BlockSpec((None, heads, kvw, LANE), lambda b, i, j: (layer, 0, 0, 0))],
        out_specs=pl.BlockSpec((tq, heads * LANE), lambda b, i, j: (b * nq + i, 0)),
        out_shape=jax.ShapeDtypeStruct((m, heads * LANE), BF16),
        scratch_shapes=[pltpu.VMEM((heads, tq, 1), F32), pltpu.VMEM((heads, tq, 1), F32),
                        pltpu.VMEM((heads, tq, kvw), F32)],
        compiler_params=_cparams("parallel", "parallel", "arbitrary"),
        name="prompt_attention",
    )(q, k, cb, w_uv)


def _key_tile_scores(lhs_main, lhs_rope, cb, xe, cs, heads, width, scale):
    tk = cb.shape[0]
    nk = heads * LANE
    r = _dot_nt(lhs_main, cb)
    kn = r[:nk].reshape(heads, LANE, tk)
    ssq = jnp.sum(kn * kn, axis=1)
    feat = (xe * cs).astype(BF16)
    sq = xe * xe
    hi = sq.astype(BF16)
    lane = lax.broadcasted_iota(jnp.int32, (tk, LANE), 1)
    hi_f = hi.astype(F32)
    sq_split = jnp.where(lane < LANE // 2, hi_f, sq - hi_f).astype(BF16)
    r2 = _dot_nt(lhs_rope, jnp.concatenate([feat, sq_split], axis=-1))
    nq = lhs_rope.shape[0] - 2 * SUBLANE
    rinv = lax.rsqrt((ssq + r2[nq:nq + heads]) * (1.0 / width) + NORM_EPS) * scale
    s = (r[nk:] + r2[:nq]).reshape(nq // heads, heads, tk) * rinv[None]
    return s.reshape(nq, tk)


def _sattn_body(pt_ref, wt_ref, qlat_ref, qg_ref, cn_ref, krn_ref, cs_ref, wuv_ref, ckv_hbm, ckr_hbm,
                y_ref, cbuf, kbuf, cb_ref, sems, *, layer, heads, width, lq, n_pages, page, tk):
    s_idx = pl.program_id(0)
    n_seq = pl.num_programs(0)
    past = n_pages * page
    nq = lq * heads
    scale = width ** -0.5
    ppt = tk // page
    n_tiles = n_pages // ppt
    hpage, htk = page // 2, tk // 2
    n_col = cbuf.shape[1]
    kvw = n_col * LANE

    def tile_copies(seq, t, slot):
        cps = []
        for i in range(ppt):
            j = t * ppt + i
            pg = pt_ref[seq, j]
            rows = pl.ds(pl.multiple_of(j * page, page), page)
            for h in range(n_col):
                cps.append(pltpu.make_async_copy(ckv_hbm.at[layer, pg, :, pl.ds(h * LANE, LANE)],
                                                 cbuf.at[slot, h, rows], sems.at[0, slot, t]))
            cps.append(pltpu.make_async_copy(ckr_hbm.at[layer, pg],
                                             kbuf.at[slot, pl.ds(pl.multiple_of(j * hpage, hpage), hpage)],
                                             sems.at[1, slot, t]))
        return cps

    def start_tile(seq, t, slot):
        for cp in tile_copies(seq, t, slot):
            cp.start()

    def wait_tile(seq, t, slot):
        for cp in tile_copies(seq, t, slot):
            cp.wait()

    slot = s_idx % 2
    has_next = s_idx + 1 < n_seq

    @pl.when(s_idx == 0)
    def _():
        def body(t, carry):
            start_tile(0, t, 0)
            return carry
        lax.fori_loop(0, n_tiles, body, 0)

    lhs_main = jnp.concatenate([wt_ref[...], qlat_ref[...]], axis=0)
    tail_r = lax.broadcasted_iota(jnp.int32, (2 * SUBLANE, 2 * LANE), 0)
    tail_c = lax.broadcasted_iota(jnp.int32, (2 * SUBLANE, 2 * LANE), 1)
    ones_rows = jnp.where((tail_r < SUBLANE) & (tail_c >= LANE), 1.0, 0.0).astype(BF16)
    lhs_rope = jnp.concatenate([qg_ref[...], ones_rows], axis=0)

    def update(carry, s, cb):
        m_prev, l_prev, acc = carry
        m_new = jnp.maximum(m_prev, jnp.max(s, axis=-1, keepdims=True))
        alpha = jnp.exp(m_prev - m_new)
        p = jnp.exp(s - m_new)
        return (m_new, alpha * l_prev + jnp.sum(p, axis=-1, keepdims=True),
                alpha * acc + _dot(p.astype(BF16), cb))

    half_lane = lax.broadcasted_iota(jnp.int32, (htk, LANE), 1) < LANE // 2

    def tile_scores(t):
        off = pl.multiple_of(t * tk, tk)
        c = jnp.concatenate(
            [jnp.concatenate([cbuf[slot, h, pl.ds(off + par, htk, stride=2), :] for h in range(n_col)], axis=1)
             for par in range(2)], axis=0)
        cb = c.astype(BF16)
        cb_ref[t % 2] = cb
        x = kbuf[slot, pl.ds(pl.multiple_of(t * htk, htk), htk), :]
        xr = pltpu.roll(x, LANE // 2, 1)
        xe = jnp.concatenate([jnp.where(half_lane, x, xr), jnp.where(half_lane, xr, x)], axis=0)
        return _key_tile_scores(lhs_main, lhs_rope, cb, xe, cs_ref[pl.ds(off, tk), :], heads, width, scale)

    def prefetch_next(t):
        @pl.when(has_next)
        def _():
            start_tile(s_idx + 1, t, 1 - slot)

    def body(t, state):
        s_cur, carry = state
        prefetch_next(t)
        wait_tile(s_idx, t + 1, slot)
        s_next = tile_scores(t + 1)
        return s_next, update(carry, s_cur, cb_ref[t % 2])

    wait_tile(s_idx, 0, slot)
    carry = (jnp.full((nq, 1), -jnp.inf, F32), jnp.zeros((nq, 1), F32), jnp.zeros((nq, kvw), F32))
    s_last, carry = lax.fori_loop(0, n_tiles - 1, body, (tile_scores(0), carry))
    prefetch_next(n_tiles - 1)
    carry = update(carry, s_last, cb_ref[(n_tiles - 1) % 2])

    pad = LANE - cn_ref.shape[0]
    cb_new = jnp.concatenate([cn_ref[...], jnp.zeros((pad, kvw), F32)], axis=0).astype(BF16)
    kr_new = jnp.concatenate([krn_ref[...], jnp.zeros((pad, krn_ref.shape[1]), F32)], axis=0)
    s_new = _key_tile_scores(lhs_main, lhs_rope, cb_new, jnp.concatenate([kr_new, kr_new], axis=-1),
                             cs_ref[pl.ds(past, LANE), :], heads, width, scale)
    key_j = lax.broadcasted_iota(jnp.int32, (nq, LANE), 1)
    q_i = lax.broadcasted_iota(jnp.int32, (nq, LANE), 0) // heads
    s_new = jnp.where((key_j < lq) & (key_j <= q_i), s_new, -jnp.inf)
    _, l_fin, acc = update(carry, s_new, cb_new)

    lat = (acc / l_fin).astype(BF16)
    res = _dot(lat, wuv_ref[...]).reshape(lq, heads, heads * LANE)
    row_h = lax.broadcasted_iota(jnp.int32, (heads, heads * LANE), 0)
    col_h = lax.broadcasted_iota(jnp.int32, (heads, heads * LANE), 1) // LANE
    y_ref[...] = jnp.sum(jnp.where((row_h == col_h)[None], res, 0.0), axis=1)


def _sample_attention(page_table, wt, qlat, qg, c_new, kr_new, cs, wuv, cache_kv, cache_kr, layer,
                      heads, width, lq, tk):
    n_seq, n_pages = page_table.shape
    page = cache_kv.shape[2]
    kvw = cache_kv.shape[3]
    krw = kr_new.shape[2]
    past = n_pages * page
    nq = lq * heads
    rows_new = c_new.shape[1]
    assert cache_kr.shape[2:] == (page // 2, LANE) and 2 * krw == LANE and tk % page == 0
    grid_spec = pltpu.PrefetchScalarGridSpec(
        num_scalar_prefetch=1,
        grid=(n_seq,),
        in_specs=[
            pl.BlockSpec((None,) + wt.shape[1:], lambda s, pt: (layer, 0, 0)),
            pl.BlockSpec((None, nq, kvw), lambda s, pt: (s, 0, 0)),
            pl.BlockSpec((None, nq, 2 * LANE), lambda s, pt: (s, 0, 0)),
            pl.BlockSpec((None, rows_new, kvw), lambda s, pt: (s, 0, 0)),
            pl.BlockSpec((None, rows_new, krw), lambda s, pt: (s, 0, 0)),
            pl.BlockSpec(cs.shape, lambda s, pt: (0, 0)),
            pl.BlockSpec((None,) + wuv.shape[1:], lambda s, pt: (layer, 0, 0)),
            pl.BlockSpec(memory_space=pl.ANY),
            pl.BlockSpec(memory_space=pl.ANY),
        ],
        out_specs=pl.BlockSpec((None, lq, heads * LANE), lambda s, pt: (s, 0, 0)),
        scratch_shapes=[pltpu.VMEM((2, kvw // LANE, past, LANE), F32), pltpu.VMEM((2, past // 2, LANE), F32),
                        pltpu.VMEM((2, tk, kvw), BF16), pltpu.SemaphoreType.DMA((2, 2, past // tk))],
    )
    return pl.pallas_call(
        functools.partial(_sattn_body, layer=layer, heads=heads, width=width, lq=lq, n_pages=n_pages,
                          page=page, tk=tk),
        grid_spec=grid_spec,
        out_shape=jax.ShapeDtypeStruct((n_seq, lq, heads * LANE), F32),
        compiler_params=_cparams("arbitrary"),
        name="sample_attention",
    )(page_table, wt, qlat, qg, c_new, kr_new, cs, wuv, cache_kv, cache_kr)


def _ln_swish(yc, g, b):
    mu = jnp.mean(yc, axis=-1, keepdims=True)
    xc = yc - mu
    y = xc * lax.rsqrt(jnp.mean(xc * xc, axis=-1, keepdims=True) + LN_EPS) * g + b
    return _silu(y)


def _conv_prompt_body(a_ref, gate_ref, w_ref, b_ref, lg_ref, lb_ref, act_ref, new_ref, xp_ref, slab_ref, *,
                      taps, chunk):
    seq = a_ref.shape[0]
    front = xp_ref.shape[0] - seq
    shift = front - (taps - 1)
    xp_ref[0:front, :] = jnp.zeros((front, xp_ref.shape[1]), F32)
    xp_ref[front:, :] = a_ref[...] * _sigmoid(gate_ref[...])
    new_ref[...] = xp_ref[seq + shift:seq + front, :]

    def body(i, carry):
        r0 = pl.multiple_of(i * chunk, chunk)
        slab_ref[...] = xp_ref[pl.ds(r0, chunk + front), :]
        acc = jnp.zeros((chunk, xp_ref.shape[1]), F32) + b_ref[...]
        for j in range(taps):
            acc = acc + slab_ref[shift + j:shift + j + chunk, :] * w_ref[j:j + 1, :]
        act_ref[pl.ds(r0, chunk), :] = _ln_swish(acc, lg_ref[...], lb_ref[...]).astype(BF16)
        return carry

    lax.fori_loop(0, seq // chunk, body, 0)


def _conv_prompt(z, cols, wts, layer, batch, seq, taps):
    ch = cols["glu_a"][1]
    a_off = cols["glu_a"][0] // ch
    g_off = cols["glu_g"][0] // ch
    front = -(-(taps - 1) // SUBLANE) * SUBLANE

    def wspec(a):
        return pl.BlockSpec((None,) + a.shape[1:], lambda b: (layer,) + (0,) * (a.ndim - 1))

    chunk = _tile(seq, 64)
    return pl.pallas_call(
        functools.partial(_conv_prompt_body, taps=taps, chunk=chunk),
        grid=(batch,),
        in_specs=[pl.BlockSpec((seq, ch), lambda b: (b, a_off)), pl.BlockSpec((seq, ch), lambda b: (b, g_off)),
                  wspec(wts["conv_w"]), wspec(wts["conv_b"]), wspec(wts["ln_g"]), wspec(wts["ln_b"])],
        out_specs=[pl.BlockSpec((seq, ch), lambda b: (b, 0)),
                   pl.BlockSpec((None, taps - 1, ch), lambda b: (b, 0, 0))],
        out_shape=[jax.ShapeDtypeStruct((batch * seq, ch), BF16),
                   jax.ShapeDtypeStruct((batch, taps - 1, ch), F32)],
        scratch_shapes=[pltpu.VMEM((front + seq, ch), F32), pltpu.VMEM((front + chunk, ch), F32)],
        compiler_params=_cparams("parallel"),
        name="conv_prompt",
    )(z, z, wts["conv_w"], wts["conv_b"], wts["ln_g"], wts["ln_b"])


def _conv_sample_body(a_ref, gate_ref, st_ref, w_ref, b_ref, lg_ref, lb_ref, act_ref, new_ref, xp_ref,
                      u_ref, yc_ref, *, taps, lq, nseq):
    hist = taps - 1
    u_ref[...] = a_ref[...] * _sigmoid(gate_ref[...])
    w = w_ref[...]
    for s in range(nseq):
        xp_ref[0:hist, :] = st_ref[s]
        xp_ref[hist:hist + lq, :] = u_ref[s * lq:(s + 1) * lq, :]
        new_ref[s] = xp_ref[lq:lq + hist, :]
        for t in range(lq):
            yc_ref[s * lq + t:s * lq + t + 1, :] = jnp.sum(xp_ref[t:t + taps, :] * w, axis=0, keepdims=True)
    act_ref[...] = _ln_swish(yc_ref[...] + b_ref[...], lg_ref[...], lb_ref[...]).astype(BF16)


def _conv_sample(z, state, cols, wts, layer, lq, taps, nseq):
    ch = cols["glu_a"][1]
    a_off = cols["glu_a"][0] // ch
    g_off = cols["glu_g"][0] // ch
    n = state.shape[1]
    rows = nseq * lq

    def wspec(a):
        return pl.BlockSpec((None,) + a.shape[1:], lambda b: (layer,) + (0,) * (a.ndim - 1))

    return pl.pallas_call(
        functools.partial(_conv_sample_body, taps=taps, lq=lq, nseq=nseq),
        grid=(n // nseq,),
        in_specs=[pl.BlockSpec((rows, ch), lambda b: (b, a_off)), pl.BlockSpec((rows, ch), lambda b: (b, g_off)),
                  pl.BlockSpec((None, nseq, taps - 1, ch), lambda b: (layer, b, 0, 0)),
                  wspec(wts["conv_w"]), wspec(wts["conv_b"]), wspec(wts["ln_g"]), wspec(wts["ln_b"])],
        out_specs=[pl.BlockSpec((rows, ch), lambda b: (b, 0)),
                   pl.BlockSpec((nseq, taps - 1, ch), lambda b: (b, 0, 0))],
        out_shape=[jax.ShapeDtypeStruct((n * lq, ch), BF16),
                   jax.ShapeDtypeStruct((n, taps - 1, ch), F32)],
        scratch_shapes=[pltpu.VMEM((-(-(taps - 1 + lq) // SUBLANE) * SUBLANE, ch), F32),
                        pltpu.VMEM((rows, ch), F32), pltpu.VMEM((rows, ch), F32)],
        compiler_params=_cparams("parallel"),
        name="conv_sample",
    )(z, z, state, wts["conv_w"], wts["conv_b"], wts["ln_g"], wts["ln_b"])


def _log_sigmoid(x):
    return jnp.minimum(x, 0.0) - jnp.log1p(jnp.exp(-jnp.abs(x)))


def _split3(x):
    h1 = x.astype(BF16)
    r1 = x - h1.astype(F32)
    h2 = r1.astype(BF16)
    h3 = (r1 - h2.astype(F32)).astype(BF16)
    return h1, h2, h3


def _gla_body(gq_ref, gk_ref, gv_ref, ga_ref, gr_ref, s0_ref, wa_ref, ba_ref, gn_ref, y_ref, sout_ref,
              st_ref, oi_ref, *, heads, dk, dv, valid, carried):
    T = GLA_TILE
    R = GLA_BLOCK
    nb = T // R
    pairs = heads // 2
    hk = heads * dk
    ti = pl.program_id(1)

    if carried:
        @pl.when(ti == 0)
        def _():
            st_ref[...] = jnp.zeros(st_ref.shape, F32)

    row = lax.broadcasted_iota(jnp.int32, (T, T), 0)
    col = lax.broadcasted_iota(jnp.int32, (T, T), 1)
    same = (row // R) == (col // R)
    tri = jnp.where(same & (col <= row), 1.0, 0.0).astype(BF16)
    ones_blk = jnp.where(same, 1.0, 0.0).astype(BF16)

    la = _log_sigmoid(_dot(ga_ref[...].astype(BF16), wa_ref[...]) + ba_ref[...]) * (1.0 / GLA_TAU)
    if valid < R:
        rr = lax.broadcasted_iota(jnp.int32, (T, hk), 0) % R
        la = jnp.where(rr < valid, la, 0.0)
    parts = jnp.concatenate(_split3(la), axis=-1)
    cums = _dot(jnp.concatenate([tri, ones_blk], axis=0), parts)
    bc = cums[:T, :hk] + cums[:T, hk:2 * hk] + cums[:T, 2 * hk:]
    bl = cums[T:, :hk] + cums[T:, hk:2 * hk] + cums[T:, 2 * hk:]
    q = gq_ref[...] * (dk ** -0.5)
    k = gk_ref[...]
    v = gv_ref[...]
    qe = q * jnp.exp(bc)
    kd = k * jnp.exp(bl - bc)
    dec = jnp.exp(bl)

    hd_row = lax.broadcasted_iota(jnp.int32, (hk, heads * dv), 0) // dk
    hd_col = lax.broadcasted_iota(jnp.int32, (hk, heads * dv), 1) // dv
    expand = jnp.where(hd_row == hd_col, 1.0, 0.0).astype(BF16)
    jrow = lax.broadcasted_iota(jnp.int32, (R, hk), 0)
    for t in range(nb):
        rs = slice(t * R, (t + 1) * R)
        b, qb, kb, vb = bc[rs], q[rs], k[rs], v[rs]
        p = []
        for i in range(R):
            e = jnp.exp(jnp.where(jrow <= i, b[i:i + 1, :] - b, -jnp.inf))
            p.append((e * qb[i:i + 1, :] * kb).astype(BF16))
        a = _dot(jnp.concatenate(p, axis=0), expand)
        for i in range(R):
            oi_ref[t * R + i:t * R + i + 1, :] = jnp.sum(a[i * R:(i + 1) * R] * vb, axis=0, keepdims=True)
    o = oi_ref[...]

    lane = lax.broadcasted_iota(jnp.int32, (T, 2 * dk), 1)
    blk = lax.broadcasted_iota(jnp.int32, (T, 2 * dk), 0) // R
    lane_blk = lax.broadcasted_iota(jnp.int32, (R, 2 * dk), 1)
    o_cols = []
    for pr in range(pairs):
        ls = slice(pr * 2 * dk, (pr + 1) * 2 * dk)
        kd_p, qe_p, dec_p = kd[:, ls], qe[:, ls], dec[:, ls]
        kv = None
        for hh in range(2):
            h = 2 * pr + hh
            kd_h = jnp.where((lane // dk) == hh, kd_p, 0.0)
            stack = jnp.concatenate([jnp.where(blk == t, kd_h, 0.0) for t in range(nb)], axis=-1).astype(BF16)
            vt = v[:, h * dv:(h + 1) * dv].T.astype(BF16)
            part = _dot(vt, stack)
            kv = part if kv is None else kv + part
        o_pair = [[], []]
        for t in range(nb):
            rs = slice(t * R, (t + 1) * R)
            if carried:
                st = st_ref[pr]
            else:
                s0 = s0_ref[t]
                st = jnp.concatenate([s0[2 * pr], s0[2 * pr + 1]], axis=0).T
            stb = st.astype(BF16)
            for hh in range(2):
                qm = jnp.where((lane_blk // dk) == hh, qe_p[rs], 0.0).astype(BF16)
                o_pair[hh].append(_dot_nt(qm, stb))
            st = st * dec_p[t * R:t * R + 1, :] + kv[:, t * 2 * dk:(t + 1) * 2 * dk]
            if carried:
                st_ref[pr] = st
            else:
                stt = st.T
                sout_ref[t, 2 * pr] = stt[:dk]
                sout_ref[t, 2 * pr + 1] = stt[dk:]
        for hh in range(2):
            o_cols.append(jnp.concatenate(o_pair[hh], axis=0))
    o = o + jnp.concatenate(o_cols, axis=-1)

    gr = gr_ref[...]
    for h in range(heads):
        cs = slice(h * dv, (h + 1) * dv)
        y_ref[:, cs] = (_rms(o[:, cs], gn_ref[...]) * _silu(gr[:, cs])).astype(BF16)

    if carried:
        @pl.when(ti == pl.num_programs(1) - 1)
        def _():
            for pr in range(pairs):
                stt = st_ref[pr].T
                sout_ref[2 * pr] = stt[:dk]
                sout_ref[2 * pr + 1] = stt[dk:]


def _gla(zg, cols, s0, wts, layer, batch, seq, heads, dk, dv, valid, carried):
    T = GLA_TILE
    m = zg.shape[0]
    hk, hv = heads * dk, heads * dv
    if carried:
        grid = (batch, seq // T)
        rowblk = lambda b, t: b * (seq // T) + t
        s_spec = pl.BlockSpec((None, heads, dk, dv), lambda b, t: (b, 0, 0, 0))
        s0_arr = jnp.zeros((1, heads, dk, dv), F32)
        s0_spec = pl.BlockSpec((1, heads, dk, dv), lambda b, t: (0, 0, 0, 0))
        n_state = batch
    else:
        nb = T // GLA_BLOCK
        grid = (1, m // T)
        rowblk = lambda b, t: t
        s_spec = pl.BlockSpec((nb, heads, dk, dv), lambda b, t: (t, 0, 0, 0))
        s0_arr = s0
        s0_spec = pl.BlockSpec((None, nb, heads, dk, dv), lambda b, t: (layer, t, 0, 0, 0))
        n_state = m // GLA_BLOCK

    def zspec(name):
        off, w = cols[name]
        return pl.BlockSpec((T, w), lambda b, t: (rowblk(b, t), off // w))

    def wspec(a):
        return pl.BlockSpec((None,) + a.shape[1:], lambda b, t: (layer,) + (0,) * (a.ndim - 1))

    return pl.pallas_call(
        functools.partial(_gla_body, heads=heads, dk=dk, dv=dv, valid=valid, carried=carried),
        grid=grid,
        in_specs=[zspec("gq"), zspec("gk"), zspec("gv"), zspec("ga"), zspec("gr"), s0_spec,
                  wspec(wts["gla_wa"]), wspec(wts["gla_ba"]), wspec(wts["gla_norm"])],
        out_specs=[pl.BlockSpec((T, hv), lambda b, t: (rowblk(b, t), 0)), s_spec],
        out_shape=[jax.ShapeDtypeStruct((m, hv), BF16), jax.ShapeDtypeStruct((n_state, heads, dk, dv), F32)],
        scratch_shapes=[pltpu.VMEM((heads // 2, dv, 2 * dk), F32), pltpu.VMEM((T, hv), F32)],
        compiler_params=_cparams("parallel", "arbitrary"),
        name="gla_carried" if carried else "gla_blocks",
    )(zg, zg, zg, zg, zg, s0_arr, wts["gla_wa"], wts["gla_ba"], wts["gla_norm"])


def _merge_body(ya_ref, yb_ref, yc_ref, g0_ref, g1_ref, g2_ref, wpa_ref, wpb_ref, wpc_ref, m_ref):
    m = _sigmoid(g0_ref[...]) * _dot(ya_ref[...], wpa_ref[...])
    m = m + _sigmoid(g1_ref[...]) * _dot(yb_ref[...], wpb_ref[...])
    m = m + _sigmoid(g2_ref[...]) * _dot(yc_ref[...], wpc_ref[...])
    m_ref[...] = m.astype(BF16)


def _merge(ya, yb, yc, z, cols, wts, layer, tm):
    m = ya.shape[0]
    d = wts["w_pa"].shape[-1]
    mg_off = cols["mg"][0] // d

    def wspec(a):
        return pl.BlockSpec((None,) + a.shape[1:], lambda i: (layer,) + (0,) * (a.ndim - 1))

    def gspec(idx):
        return pl.BlockSpec((tm, d), lambda i: (i, mg_off + idx))

    def yspec(a):
        return pl.BlockSpec((tm, a.shape[1]), lambda i: (i, 0))

    return pl.pallas_call(
        _merge_body,
        grid=(m // tm,),
        in_specs=[yspec(ya), yspec(yb), yspec(yc), gspec(0), gspec(1), gspec(2),
                  wspec(wts["w_pa"]), wspec(wts["w_pb"]), wspec(wts["w_pc"])],
        out_specs=pl.BlockSpec((tm, d), lambda i: (i, 0)),
        out_shape=jax.ShapeDtypeStruct((m, d), BF16),
        compiler_params=_cparams("parallel"),
        name="merge",
    )(ya, yb, yc, z, z, z, wts["w_pa"], wts["w_pb"], wts["w_pc"])


def _out_proj_body(x_ref, m_ref, w_ref, o_ref):
    o_ref[...] = x_ref[...] + _dot(m_ref[...], w_ref[...])


def _out_proj(x, mm, w, layer, tm, tn):
    m, d = x.shape
    return pl.pallas_call(
        _out_proj_body,
        grid=(m // tm, d // tn),
        in_specs=[pl.BlockSpec((tm, tn), lambda i, j: (i, j)),
                  pl.BlockSpec((tm, d), lambda i, j: (i, 0)),
                  pl.BlockSpec((None, d, tn), lambda i, j: (layer, 0, j))],
        out_specs=pl.BlockSpec((tm, tn), lambda i, j: (i, j)),
        out_shape=jax.ShapeDtypeStruct((m, d), F32),
        compiler_params=_cparams("parallel", "parallel"),
        name="out_proj",
    )(x, mm, w)


def _ffn_body(x_ref, g_ref, wg_ref, wu_ref, wo_ref, o_ref, h_ref, acc_ref):
    f = pl.program_id(1)

    @pl.when(f == 0)
    def _():
        h_ref[...] = _rms(x_ref[...], g_ref[...]).astype(BF16)
        acc_ref[...] = jnp.zeros(acc_ref.shape, F32)

    h = h_ref[...]
    act = (_silu(_dot(h, wg_ref[...])) * _dot(h, wu_ref[...])).astype(BF16)
    acc_ref[...] += _dot(act, wo_ref[...])

    @pl.when(f == pl.num_programs(1) - 1)
    def _():
        o_ref[...] = x_ref[...] + acc_ref[...]


def _ffn(x, g, w_in, w_out, layer, tm, tf):
    m, d = x.shape
    dff = w_out.shape[1]
    nf = dff // tf
    return pl.pallas_call(
        _ffn_body,
        grid=(m // tm, nf),
        in_specs=[pl.BlockSpec((tm, d), lambda i, f: (i, 0)),
                  pl.BlockSpec((None, 1, d), lambda i, f: (layer, 0, 0)),
                  pl.BlockSpec((None, d, tf), lambda i, f: (layer, 0, f)),
                  pl.BlockSpec((None, d, tf), lambda i, f: (layer, 0, nf + f)),
                  pl.BlockSpec((None, tf, d), lambda i, f: (layer, f, 0))],
        out_specs=pl.BlockSpec((tm, d), lambda i, f: (i, 0)),
        out_shape=jax.ShapeDtypeStruct((m, d), F32),
        scratch_shapes=[pltpu.VMEM((tm, d), BF16), pltpu.VMEM((tm, d), F32)],
        compiler_params=_cparams("parallel", "arbitrary"),
        name="ffn",
    )(x, g, w_in, w_in, w_out)


def _rope_angles(pos):
    half = 32
    inv = jnp.exp(-math.log(ROPE_BASE) * jnp.arange(half, dtype=F32) / half)
    ang = pos.astype(F32)[:, None] * inv[None, :]
    return jnp.cos(ang), jnp.sin(ang)


def _rope_tables(pos):
    cos, sin = _rope_angles(pos)
    z = jnp.zeros((pos.shape[0], LANE // 2), F32)
    return jnp.concatenate([cos, cos, z], -1), jnp.concatenate([-sin, sin, z], -1)


def _pad_cols(a, width):
    return jnp.pad(a, [(0, 0)] * (a.ndim - 1) + [(0, width - a.shape[-1])])


def kernel(x_prompt, x_sample, cache_kv, cache_krope, state_conv, state_gla, page_table, attn_norm, w_in, cq_norm, w_uq, q_norm, kv_norm, w_uk, k_norm, w_uv, w_pa, conv_w, conv_b, conv_ln_g, conv_ln_b, w_pb, gla_wa, gla_ba, gla_norm, w_pc, w_o, ffn_norm, w_ffn_in, w_ffn_out):
    depth, d_model, _ = w_in.shape
    batch, seq, _ = x_prompt.shape
    n_seq, lq, _ = x_sample.shape
    q_lora = cq_norm.shape[1]
    kv_lora = kv_norm.shape[1]
    heads, qk_head = w_uq.shape[2], w_uq.shape[3]
    qk_nope = w_uk.shape[3]
    qk_rope = qk_head - qk_nope
    conv_ch = conv_b.shape[1]
    taps = conv_w.shape[1]
    g_heads, g_dk, g_dv = state_gla.shape[2], state_gla.shape[3], state_gla.shape[4]
    g_rank = gla_wa.shape[1]
    n_pages, page = page_table.shape[1], cache_kv.shape[2]
    past = n_pages * page
    assert (qk_nope, qk_rope, kv_lora) == (LANE, LANE // 2, 2 * LANE) and g_heads % 2 == 0 and 2 * g_dk == LANE

    ref_sizes = dict(cq=q_lora, ckv=kv_lora, kr=qk_rope, glu=2 * conv_ch, gq=g_heads * g_dk, gk=g_heads * g_dk,
                     gv=g_heads * g_dv, ga=g_rank, gr=g_heads * g_dv, mg=3 * d_model)
    ref_off, off = {}, 0
    for name, size in ref_sizes.items():
        ref_off[name] = off
        off += size
    pieces = [("mg", ref_off["mg"], 3 * d_model, 3 * d_model),
              ("glu_a", ref_off["glu"], conv_ch, conv_ch), ("glu_g", ref_off["glu"] + conv_ch, conv_ch, conv_ch),
              ("cq", ref_off["cq"], q_lora, q_lora), ("gv", ref_off["gv"], g_heads * g_dv, g_heads * g_dv),
              ("gr", ref_off["gr"], g_heads * g_dv, g_heads * g_dv), ("ckv", ref_off["ckv"], kv_lora, kv_lora),
              ("gq", ref_off["gq"], g_heads * g_dk, g_heads * g_dk), ("gk", ref_off["gk"], g_heads * g_dk, g_heads * g_dk),
              ("kr", ref_off["kr"], qk_rope, LANE), ("ga", ref_off["ga"], g_rank, LANE)]
    cols, off, packed = {}, 0, []
    for name, src, size, width in pieces:
        cols[name] = (off, width)
        packed.append(_pad_cols(w_in[:, :, src:src + size], width))
        off += width
    cols["mg"] = (0, d_model)
    w_in_p = jnp.concatenate(packed, axis=-1).astype(BF16)
    n_in = w_in_p.shape[-1]

    wuq_p = jnp.concatenate([w_uq[..., :qk_nope], _pad_cols(w_uq[..., qk_nope:], LANE)], axis=-1)
    wuq_p = wuq_p.reshape(depth, q_lora, heads * HEAD_PAD).astype(BF16)
    pad_gain = lambda g: jnp.concatenate([g[:, :qk_nope], _pad_cols(g[:, qk_nope:], LANE)], -1)[:, None, :]
    half = qk_rope // 2
    g_r1, g_r2 = k_norm[:, qk_nope:qk_nope + half], k_norm[:, qk_nope + half:]
    zeros_h = jnp.zeros_like(g_r1)
    attn_w = dict(
        g_cq=cq_norm[:, None, :], w_uq=wuq_p, g_q=pad_gain(q_norm), g_kv=kv_norm[:, None, :],
        w_uk=w_uk.reshape(depth, kv_lora, heads * qk_nope).astype(BF16), g_k=pad_gain(k_norm),
        w_uk3=jnp.transpose(w_uk, (0, 2, 1, 3)).astype(BF16),
        g_t1=jnp.concatenate([g_r1, g_r2, zeros_h, zeros_h], -1)[:, None, :],
        g_t2=jnp.concatenate([zeros_h, zeros_h, g_r1, -g_r2], -1)[:, None, :],
    )
    w_ukT = jnp.transpose(w_uk.reshape(depth, kv_lora, heads * qk_nope), (0, 2, 1)).astype(BF16)
    w_uv_heads = jnp.transpose(w_uv, (0, 2, 1, 3)).astype(BF16)
    w_uv_flat = w_uv.reshape(depth, kv_lora, heads * w_uv.shape[3]).astype(BF16)
    conv_wts = dict(conv_w=conv_w, conv_b=conv_b[:, None, :], ln_g=conv_ln_g[:, None, :], ln_b=conv_ln_b[:, None, :])
    gla_wts = dict(gla_wa=jnp.pad(gla_wa, ((0, 0), (0, LANE - g_rank), (0, 0))).astype(BF16),
                   gla_ba=gla_ba[:, None, :], gla_norm=gla_norm[:, None, :])
    merge_wts = dict(w_pa=w_pa.astype(BF16), w_pb=w_pb.astype(BF16), w_pc=w_pc.astype(BF16))
    w_o_b = w_o.astype(BF16)
    w_ffn_in_b = w_ffn_in.astype(BF16)
    w_ffn_out_b = w_ffn_out.astype(BF16)
    g_attn = attn_norm[:, None, :]
    g_ffn = ffn_norm[:, None, :]

    cos_p, sin_p = _rope_tables(jnp.arange(seq, dtype=jnp.int32))
    cos_s, sin_s = _rope_tables(past + jnp.arange(lq, dtype=jnp.int32))
    cos_s, sin_s = jnp.tile(cos_s, (n_seq, 1)), jnp.tile(sin_s, (n_seq, 1))
    tk_s = _tile(past, 512)
    key_pos = jnp.arange(past, dtype=jnp.int32).reshape(past // tk_s, tk_s // 2, 2)
    key_pos = jnp.concatenate([jnp.transpose(key_pos, (0, 2, 1)).reshape(past),
                               past + jnp.arange(LANE, dtype=jnp.int32)])
    kc, ks = _rope_angles(key_pos)
    key_cs = jnp.concatenate([kc, kc, ks, ks], axis=-1)
    cache_kr2 = cache_krope.reshape(depth, cache_krope.shape[1], page // 2, 2 * qk_rope)

    mp, ms = batch * seq, n_seq * lq
    tm_p = _tile(mp, 1024)
    tn_in = _tile(n_in, 512)
    tf = _tile(w_ffn_out.shape[1], 512)
    gla_cols = {k: cols[k] for k in ("gq", "gk", "gv", "ga", "gr")}
    gla_lo = min(v[0] for v in gla_cols.values())

    xp = x_prompt.reshape(mp, d_model)
    xs = x_sample.reshape(ms, d_model)
    outs = {k: [] for k in ("ckv_p", "kr_p", "conv_p", "gla_p", "ckv_s", "kr_s", "conv_s", "gla_s")}
    kr_off = cols["kr"][0]
    for l in range(depth):
        z = _norm_matmul(xp, g_attn, w_in_p, l, tm_p, tn_in)
        q, k, ckv, cb = _qk_prompt(z, cols, cos_p, sin_p, attn_w, l, _tile(seq, 256), heads, qk_head)
        ya = _prompt_attention(q, k, cb, w_uv_heads, l, batch, seq, heads, qk_head, _tile(seq, 512), _tile(seq, 512))
        yb, conv_new = _conv_prompt(z, cols, conv_wts, l, batch, seq, taps)
        yc, gla_new = _gla(z, gla_cols, None, gla_wts, l, batch, seq, g_heads, g_dk, g_dv, GLA_BLOCK, True)
        mm = _merge(ya, yb, yc, z, cols, merge_wts, l, _tile(mp, 256))
        x1 = _out_proj(xp, mm, w_o_b, l, tm_p, _tile(d_model, 512))
        xp = _ffn(x1, g_ffn, w_ffn_in_b, w_ffn_out_b, l, _tile(mp, 512), tf)
        outs["ckv_p"].append(ckv.reshape(batch, seq, kv_lora))
        outs["kr_p"].append(z[:, kr_off:kr_off + qk_rope].reshape(batch, seq, qk_rope))
        outs["conv_p"].append(conv_new)
        outs["gla_p"].append(gla_new)

        z = _norm_matmul(xs, g_attn, w_in_p, l, ms, tn_in)
        qlat, qg, ckv = _q_sample(z, cols, cos_s, sin_s, attn_w, l, heads, qk_head)
        kr_new = z[:, kr_off:kr_off + qk_rope]
        rows_new = -(-lq // SUBLANE) * SUBLANE
        c_new = jnp.pad(ckv.reshape(n_seq, lq, kv_lora), ((0, 0), (0, rows_new - lq), (0, 0)))
        kr_pad = jnp.pad(kr_new.reshape(n_seq, lq, qk_rope), ((0, 0), (0, rows_new - lq), (0, 0)))
        ya = _sample_attention(page_table, w_ukT, qlat.reshape(n_seq, lq * heads, kv_lora),
                               qg.reshape(n_seq, lq * heads, 2 * LANE), c_new, kr_pad, key_cs, w_uv_flat,
                               cache_kv, cache_kr2, l, heads, qk_head, lq, tk_s)
        ya = ya.reshape(ms, heads * LANE).astype(BF16)
        yb, conv_new = _conv_sample(z, state_conv, cols, conv_wts, l, lq, taps, SUBLANE)
        zg = z[:, gla_lo:]
        zg = jnp.pad(zg.reshape(n_seq, lq, -1), ((0, 0), (0, GLA_BLOCK - lq), (0, 0))).reshape(n_seq * GLA_BLOCK, -1)
        zg_cols = {k: (v[0] - gla_lo, v[1]) for k, v in gla_cols.items()}
        yc, gla_new = _gla(zg, zg_cols, state_gla, gla_wts, l, 1, GLA_BLOCK, g_heads, g_dk, g_dv, lq, False)
        yc = yc.reshape(n_seq, GLA_BLOCK, -1)[:, :lq].reshape(ms, -1)
        mm = _merge(ya, yb, yc, z, cols, merge_wts, l, _tile(ms, 256))
        x1 = _out_proj(xs, mm, w_o_b, l, ms, _tile(d_model, 512))
        xs = _ffn(x1, g_ffn, w_ffn_in_b, w_ffn_out_b, l, ms, tf)
        outs["ckv_s"].append(ckv.reshape(n_seq, lq, kv_lora))
        outs["kr_s"].append(kr_new.reshape(n_seq, lq, qk_rope))
        outs["conv_s"].append(conv_new)
        outs["gla_s"].append(gla_new)

    stack = lambda key: jnp.stack(outs[key])
    return (xp.reshape(batch, seq, d_model), xs.reshape(n_seq, lq, d_model),
            stack("ckv_p"), stack("kr_p"), stack("conv_p"), stack("gla_p"),
            stack("ckv_s"), stack("kr_s"), stack("conv_s"), stack("gla_s"))
```

```python
import functools
import math

import jax
import jax.numpy as jnp
from jax import lax
from jax.experimental import pallas as pl
from jax.experimental.pallas import tpu as pltpu

F32 = jnp.float32
BF16 = jnp.bfloat16

NORM_EPS = 1e-6
LN_EPS = 1e-5
ROPE_BASE = 10000.0
GLA_TAU = 16.0

LANE = 128
SUBLANE = 8
VMEM_LIMIT = 56 * 1024 * 1024
GLA_BLOCK = 16
GLA_TILE = 128
HEAD_PAD = 256


def _cparams(*sem):
    return pltpu.CompilerParams(dimension_semantics=sem, vmem_limit_bytes=VMEM_LIMIT)


def _tile(n, pref):
    if n <= pref:
        return n
    t = pref
    while n % t:
        t -= SUBLANE
    return t


def _rms(x, g, eps=NORM_EPS):
    return x * lax.rsqrt(jnp.mean(x * x, axis=-1, keepdims=True) + eps) * g


def _dot(a, b):
    return jnp.dot(a, b, preferred_element_type=F32)


def _dot_nt(a, b):
    return lax.dot_general(a, b, (((1,), (1,)), ((), ())), preferred_element_type=F32)


def _sigmoid(x):
    return 1.0 / (1.0 + jnp.exp(-x))


def _silu(x):
    return x * _sigmoid(x)


def _rope_lanes(x, cos_t, sin_t):
    return x * cos_t + (pltpu.roll(x, 96, 1) + pltpu.roll(x, 32, 1)) * sin_t


def _norm_matmul_body(x_ref, g_ref, w_ref, o_ref, h_ref):
    @pl.when(pl.program_id(1) == 0)
    def _():
        h_ref[...] = _rms(x_ref[...], g_ref[...]).astype(BF16)

    o_ref[...] = _dot(h_ref[...], w_ref[...])


def _norm_matmul(x, g, w, layer, tm, tn):
    m, k = x.shape
    n = w.shape[-1]
    return pl.pallas_call(
        _norm_matmul_body,
        grid=(m // tm, n // tn),
        in_specs=[
            pl.BlockSpec((tm, k), lambda i, j: (i, 0)),
            pl.BlockSpec((None, 1, k), lambda i, j: (layer, 0, 0)),
            pl.BlockSpec((None, k, tn), lambda i, j: (layer, 0, j)),
        ],
        out_specs=pl.BlockSpec((tm, tn), lambda i, j: (i, j)),
        out_shape=jax.ShapeDtypeStruct((m, n), F32),
        scratch_shapes=[pltpu.VMEM((tm, k), BF16)],
        compiler_params=_cparams("parallel", "arbitrary"),
        name="in_proj",
    )(x, g, w)


def _head_norm_q(qraw, h, gq_ref, cos_t, sin_t, width):
    lo = h * HEAD_PAD
    qn = qraw[:, lo:lo + LANE]
    qr = qraw[:, lo + LANE:lo + HEAD_PAD]
    ss = jnp.sum(qn * qn, axis=-1, keepdims=True) + jnp.sum(qr * qr, axis=-1, keepdims=True)
    rinv = lax.rsqrt(ss * (1.0 / width) + NORM_EPS)
    qn = qn * rinv * gq_ref[:, :LANE]
    qr = _rope_lanes(qr * rinv * gq_ref[:, LANE:], cos_t, sin_t)
    return qn, qr


def _qk_prompt_body(zcq_ref, zckv_ref, zkr_ref, cos_ref, sin_ref, gcq_ref, wuq_ref, gq_ref, gkv_ref,
                    wuk_ref, gk_ref, q_ref, k_ref, ckv_ref, cbt_ref, *, heads, width):
    cos_t = cos_ref[...]
    sin_t = sin_ref[...]
    cq = _rms(zcq_ref[...], gcq_ref[...]).astype(BF16)
    qraw = _dot(cq, wuq_ref[...])
    c = _rms(zckv_ref[...], gkv_ref[...])
    ckv_ref[...] = c
    cbt_ref[...] = c.T.astype(BF16)
    kn = _dot(c.astype(BF16), wuk_ref[...])
    krp = zkr_ref[...]
    kr_ss = jnp.sum(krp * krp, axis=-1, keepdims=True)
    scale = width ** -0.5
    for h in range(heads):
        qn, qr = _head_norm_q(qraw, h, gq_ref, cos_t, sin_t, width)
        lo = h * HEAD_PAD
        q_ref[:, lo:lo + LANE] = (qn * scale).astype(BF16)
        q_ref[:, lo + LANE:lo + HEAD_PAD] = (qr * scale).astype(BF16)
        kh = kn[:, h * LANE:(h + 1) * LANE]
        rk = lax.rsqrt((jnp.sum(kh * kh, axis=-1, keepdims=True) + kr_ss) * (1.0 / width) + NORM_EPS)
        k_ref[:, lo:lo + LANE] = (kh * rk * gk_ref[:, :LANE]).astype(BF16)
        k_ref[:, lo + LANE:lo + HEAD_PAD] = _rope_lanes(krp * rk * gk_ref[:, LANE:], cos_t, sin_t).astype(BF16)


def _qk_prompt(z, cols, cos_t, sin_t, wts, layer, tm, heads, width):
    m = z.shape[0]
    n_tab = cos_t.shape[0] // tm
    hp = heads * HEAD_PAD

    def zspec(name):
        off, w = cols[name]
        return pl.BlockSpec((tm, w), lambda i: (i, off // w))

    def wspec(a):
        return pl.BlockSpec((None,) + a.shape[1:], lambda i: (layer,) + (0,) * (a.ndim - 1))

    tab = pl.BlockSpec((tm, LANE), lambda i: (i % n_tab, 0))
    return pl.pallas_call(
        functools.partial(_qk_prompt_body, heads=heads, width=width),
        grid=(m // tm,),
        in_specs=[zspec("cq"), zspec("ckv"), zspec("kr"), tab, tab,
                  wspec(wts["g_cq"]), wspec(wts["w_uq"]), wspec(wts["g_q"]), wspec(wts["g_kv"]),
                  wspec(wts["w_uk"]), wspec(wts["g_k"])],
        out_specs=[pl.BlockSpec((tm, hp), lambda i: (i, 0)), pl.BlockSpec((tm, hp), lambda i: (i, 0)),
                   pl.BlockSpec((tm, cols["ckv"][1]), lambda i: (i, 0)),
                   pl.BlockSpec((cols["ckv"][1], tm), lambda i: (0, i))],
        out_shape=[jax.ShapeDtypeStruct((m, hp), BF16), jax.ShapeDtypeStruct((m, hp), BF16),
                   jax.ShapeDtypeStruct((m, cols["ckv"][1]), F32),
                   jax.ShapeDtypeStruct((cols["ckv"][1], m), BF16)],
        compiler_params=_cparams("parallel"),
        name="qk_prompt",
    )(z, z, z, cos_t, sin_t, wts["g_cq"], wts["w_uq"], wts["g_q"], wts["g_kv"], wts["w_uk"], wts["g_k"])


def _q_sample_body(zcq_ref, zckv_ref, cos_ref, sin_ref, gcq_ref, wuq_ref, gq_ref, gkv_ref, wuk3_ref,
                   gk_ref, gt1_ref, gt2_ref, qlat_ref, qg_ref, ckv_ref, *, heads, width):
    cos_t = cos_ref[...]
    sin_t = sin_ref[...]
    cq = _rms(zcq_ref[...], gcq_ref[...]).astype(BF16)
    qraw = _dot(cq, wuq_ref[...])
    ckv_ref[...] = _rms(zckv_ref[...], gkv_ref[...])
    zero = jnp.zeros((qraw.shape[0], LANE), BF16)
    for h in range(heads):
        qn, qr = _head_norm_q(qraw, h, gq_ref, cos_t, sin_t, width)
        lo = h * HEAD_PAD
        qlat_ref[:, lo:lo + HEAD_PAD] = _dot_nt((qn * gk_ref[:, :LANE]).astype(BF16), wuk3_ref[h]).astype(BF16)
        feat = qr * gt1_ref[...] + (pltpu.roll(qr, 32, 1) + pltpu.roll(qr, 96, 1)) * gt2_ref[...]
        qg_ref[:, lo:lo + LANE] = feat.astype(BF16)
        qg_ref[:, lo + LANE:lo + HEAD_PAD] = zero


def _q_sample(z, cols, cos_t, sin_t, wts, layer, heads, width):
    m = z.shape[0]
    tm = m
    hp = heads * HEAD_PAD

    def zspec(name):
        off, w = cols[name]
        return pl.BlockSpec((tm, w), lambda i: (i, off // w))

    def wspec(a):
        return pl.BlockSpec((None,) + a.shape[1:], lambda i: (layer,) + (0,) * (a.ndim - 1))

    tab = pl.BlockSpec((tm, LANE), lambda i: (i, 0))
    kvw = cols["ckv"][1]
    return pl.pallas_call(
        functools.partial(_q_sample_body, heads=heads, width=width),
        grid=(m // tm,),
        in_specs=[zspec("cq"), zspec("ckv"), tab, tab,
                  wspec(wts["g_cq"]), wspec(wts["w_uq"]), wspec(wts["g_q"]), wspec(wts["g_kv"]),
                  wspec(wts["w_uk3"]), wspec(wts["g_k"]), wspec(wts["g_t1"]), wspec(wts["g_t2"])],
        out_specs=[pl.BlockSpec((tm, hp), lambda i: (i, 0)), pl.BlockSpec((tm, hp), lambda i: (i, 0)),
                   pl.BlockSpec((tm, kvw), lambda i: (i, 0))],
        out_shape=[jax.ShapeDtypeStruct((m, hp), BF16), jax.ShapeDtypeStruct((m, hp), BF16),
                   jax.ShapeDtypeStruct((m, kvw), F32)],
        compiler_params=_cparams("parallel"),
        name="q_sample",
    )(z, z, cos_t, sin_t, wts["g_cq"], wts["w_uq"], wts["g_q"], wts["g_kv"], wts["w_uk3"], wts["g_k"],
      wts["g_t1"], wts["g_t2"])


def _pattn_body(q_ref, k_ref, ct_ref, wuv_ref, y_ref, m_ref, l_ref, acc_ref, *, heads, tq, tk):
    qi = pl.program_id(1)
    ki = pl.program_id(2)

    @pl.when(ki == 0)
    def _():
        m_ref[...] = jnp.full(m_ref.shape, -jnp.inf, F32)
        l_ref[...] = jnp.zeros(l_ref.shape, F32)
        acc_ref[...] = jnp.zeros(acc_ref.shape, F32)

    def tile(masked):
        ct = ct_ref[...]
        if masked:
            kpos = ki * tk + lax.broadcasted_iota(jnp.int32, (tk, tq), 0)
            qpos = qi * tq + lax.broadcasted_iota(jnp.int32, (tk, tq), 1)
            visible = kpos <= qpos
        for h in range(heads):
            lo = h * HEAD_PAD
            s = _dot_nt(k_ref[:, lo:lo + HEAD_PAD], q_ref[:, lo:lo + HEAD_PAD])
            if masked:
                s = jnp.where(visible, s, -jnp.inf)
            m_prev = m_ref[h]
            m_new = jnp.maximum(m_prev, jnp.max(s, axis=0, keepdims=True))
            alpha = jnp.exp(m_prev - m_new)
            p = jnp.exp(s - m_new)
            l_ref[h] = alpha * l_ref[h] + jnp.sum(p, axis=0, keepdims=True)
            acc_ref[h] = alpha * acc_ref[h] + _dot(ct, p.astype(BF16))
            m_ref[h] = m_new

    first_q, last_q = qi * tq, qi * tq + tq - 1
    first_k, last_k = ki * tk, ki * tk + tk - 1

    @pl.when(last_k <= first_q)
    def _():
        tile(False)

    @pl.when((last_k > first_q) & (first_k <= last_q))
    def _():
        tile(True)

    @pl.when(ki == pl.num_programs(2) - 1)
    def _():
        for h in range(heads):
            lat = (acc_ref[h] / l_ref[h]).T.astype(BF16)
            y_ref[:, h * LANE:(h + 1) * LANE] = _dot(lat, wuv_ref[h]).astype(BF16)


def _prompt_attention(q, k, cb_t, w_uv, layer, batch, seq, heads, tq, tk):
    m = q.shape[0]
    nq, nk = seq // tq, seq // tk
    hp = heads * HEAD_PAD
    kvw = cb_t.shape[0]

    def kv_blk(b, i, j):
        return b * nk + jnp.minimum(j, (i * tq + tq - 1) // tk)

    return pl.pallas_call(
        functools.partial(_pattn_body, heads=heads, tq=tq, tk=tk),
        grid=(batch, nq, nk),
        in_specs=[pl.BlockSpec((tq, hp), lambda b, i, j: (b * nq + i, 0)),
                  pl.BlockSpec((tk, hp), lambda b, i, j: (kv_blk(b, i, j), 0)),
                  pl.BlockSpec((kvw, tk), lambda b, i, j: (0, kv_blk(b, i, j))),
                  pl.BlockSpec((None, heads, kvw, LANE), lambda b, i, j: (layer, 0, 0, 0))],
        out_specs=pl.BlockSpec((tq, heads * LANE), lambda b, i, j: (b * nq + i, 0)),
        out_shape=jax.ShapeDtypeStruct((m, heads * LANE), BF16),
        scratch_shapes=[pltpu.VMEM((heads, 1, tq), F32), pltpu.VMEM((heads, 1, tq), F32),
                        pltpu.VMEM((heads, kvw, tq), F32)],
        compiler_params=_cparams("parallel", "parallel", "arbitrary"),
        name="prompt_attention",
    )(q, k, cb_t, w_uv)


def _key_tile_scores(lhs_main, lhs_rope, cb, kr_t, cs_t, heads, width, scale):
    tk = cb.shape[0]
    nk = heads * LANE
    r = _dot_nt(lhs_main, cb)
    kn = r[:nk].reshape(heads, LANE, tk)
    ssq = jnp.sum(kn * kn, axis=1)
    feat = (jnp.concatenate([kr_t, kr_t], axis=0) * cs_t).astype(BF16)
    sq = kr_t * kr_t
    hi = sq.astype(BF16)
    lo = (sq - hi.astype(F32)).astype(BF16)
    r2 = _dot(lhs_rope, jnp.concatenate([feat, hi, lo], axis=0))
    nq = lhs_rope.shape[0] - 2 * SUBLANE
    rinv = lax.rsqrt((ssq + r2[nq:nq + heads]) * (1.0 / width) + NORM_EPS) * scale
    s = (r[nk:] + r2[:nq]).reshape(nq // heads, heads, tk) * rinv[None]
    return s.reshape(nq, tk)


def _sattn_body(pt_ref, wt_ref, qlat_ref, qg_ref, cn_ref, krn_ref, cs_ref, csn_ref, wuv_ref, ckv_hbm, ckr_hbm,
                y_ref, cbuf, kbuf, cb_ref, sems, *, layer, heads, width, lq, n_pages, page, tk):
    s_idx = pl.program_id(0)
    n_seq = pl.num_programs(0)
    nq = lq * heads
    scale = width ** -0.5
    ppt = tk // page
    n_tiles = n_pages // ppt
    kvw = cbuf.shape[-1]

    def tile_copies(seq, t, slot):
        cps = []
        for i in range(ppt):
            pg = pt_ref[seq, t * ppt + i]
            cps.append(pltpu.make_async_copy(ckv_hbm.at[layer, pg], cbuf.at[slot, t, pl.ds(i * page, page)],
                                             sems.at[0, slot, t]))
            cps.append(pltpu.make_async_copy(ckr_hbm.at[layer, pg], kbuf.at[slot, t, :, pl.ds(i * page, page)],
                                             sems.at[1, slot, t]))
        return cps

    def start_tile(seq, t, slot):
        for cp in tile_copies(seq, t, slot):
            cp.start()

    def wait_tile(seq, t, slot):
        for cp in tile_copies(seq, t, slot):
            cp.wait()

    slot = s_idx % 2
    has_next = s_idx + 1 < n_seq

    @pl.when(s_idx == 0)
    def _():
        def body(t, carry):
            start_tile(0, t, 0)
            return carry
        lax.fori_loop(0, n_tiles, body, 0)

    lhs_main = jnp.concatenate([wt_ref[...], qlat_ref[...]], axis=0)
    tail_r = lax.broadcasted_iota(jnp.int32, (2 * SUBLANE, 2 * LANE), 0)
    tail_c = lax.broadcasted_iota(jnp.int32, (2 * SUBLANE, 2 * LANE), 1)
    ones_rows = jnp.where((tail_r < SUBLANE) & (tail_c >= LANE), 1.0, 0.0).astype(BF16)
    lhs_rope = jnp.concatenate([qg_ref[...], ones_rows], axis=0)

    def update(carry, s, cb):
        m_prev, l_prev, acc = carry
        m_new = jnp.maximum(m_prev, jnp.max(s, axis=-1, keepdims=True))
        alpha = jnp.exp(m_prev - m_new)
        p = jnp.exp(s - m_new)
        return (m_new, alpha * l_prev + jnp.sum(p, axis=-1, keepdims=True),
                alpha * acc + _dot(p.astype(BF16), cb))

    def tile_scores(t):
        cb = cbuf[slot, t].astype(BF16)
        cb_ref[t % 2] = cb
        return _key_tile_scores(lhs_main, lhs_rope, cb, kbuf[slot, t], cs_ref[t], heads, width, scale)

    def prefetch_next(t):
        @pl.when(has_next)
        def _():
            start_tile(s_idx + 1, t, 1 - slot)

    def body(t, state):
        s_cur, carry = state
        prefetch_next(t)
        wait_tile(s_idx, t + 1, slot)
        s_next = tile_scores(t + 1)
        return s_next, update(carry, s_cur, cb_ref[t % 2])

    wait_tile(s_idx, 0, slot)
    carry = (jnp.full((nq, 1), -jnp.inf, F32), jnp.zeros((nq, 1), F32), jnp.zeros((nq, kvw), F32))
    s_last, carry = lax.fori_loop(0, n_tiles - 1, body, (tile_scores(0), carry))
    prefetch_next(n_tiles - 1)
    carry = update(carry, s_last, cb_ref[(n_tiles - 1) % 2])

    pad = LANE - cn_ref.shape[0]
    cb_new = jnp.concatenate([cn_ref[...], jnp.zeros((pad, kvw), F32)], axis=0).astype(BF16)
    s_new = _key_tile_scores(lhs_main, lhs_rope, cb_new, krn_ref[...], csn_ref[...], heads, width, scale)
    key_j = lax.broadcasted_iota(jnp.int32, (nq, LANE), 1)
    q_i = lax.broadcasted_iota(jnp.int32, (nq, LANE), 0) // heads
    s_new = jnp.where((key_j < lq) & (key_j <= q_i), s_new, -jnp.inf)
    _, l_fin, acc = update(carry, s_new, cb_new)

    lat = (acc / l_fin).astype(BF16)
    res = _dot(lat, wuv_ref[...]).reshape(lq, heads, heads * LANE)
    row_h = lax.broadcasted_iota(jnp.int32, (heads, heads * LANE), 0)
    col_h = lax.broadcasted_iota(jnp.int32, (heads, heads * LANE), 1) // LANE
    y_ref[...] = jnp.sum(jnp.where((row_h == col_h)[None], res, 0.0), axis=1)


def _sample_attention(page_table, wt, qlat, qg, c_new, kr_new_t, cs_t, cs_new_t, wuv, cache_kv, cache_kr_t,
                      layer, heads, width, lq, tk):
    n_seq, n_pages = page_table.shape
    page = cache_kv.shape[2]
    kvw = cache_kv.shape[3]
    krw = cache_kr_t.shape[2]
    n_tiles = n_pages * page // tk
    nq = lq * heads
    rows_new = c_new.shape[1]
    assert cache_kr_t.shape[3] == page and tk % page == 0 and page == LANE
    grid_spec = pltpu.PrefetchScalarGridSpec(
        num_scalar_prefetch=1,
        grid=(n_seq,),
        in_specs=[
            pl.BlockSpec((None,) + wt.shape[1:], lambda s, pt: (layer, 0, 0)),
            pl.BlockSpec((None, nq, kvw), lambda s, pt: (s, 0, 0)),
            pl.BlockSpec((None, nq, 2 * LANE), lambda s, pt: (s, 0, 0)),
            pl.BlockSpec((None, rows_new, kvw), lambda s, pt: (s, 0, 0)),
            pl.BlockSpec((None, krw, LANE), lambda s, pt: (s, 0, 0)),
            pl.BlockSpec(cs_t.shape, lambda s, pt: (0, 0, 0)),
            pl.BlockSpec(cs_new_t.shape, lambda s, pt: (0, 0)),
            pl.BlockSpec((None,) + wuv.shape[1:], lambda s, pt: (layer, 0, 0)),
            pl.BlockSpec(memory_space=pl.ANY),
            pl.BlockSpec(memory_space=pl.ANY),
        ],
        out_specs=pl.BlockSpec((None, lq, heads * LANE), lambda s, pt: (s, 0, 0)),
        scratch_shapes=[pltpu.VMEM((2, n_tiles, tk, kvw), F32), pltpu.VMEM((2, n_tiles, krw, tk), F32),
                        pltpu.VMEM((2, tk, kvw), BF16), pltpu.SemaphoreType.DMA((2, 2, n_tiles))],
    )
    return pl.pallas_call(
        functools.partial(_sattn_body, layer=layer, heads=heads, width=width, lq=lq, n_pages=n_pages,
                          page=page, tk=tk),
        grid_spec=grid_spec,
        out_shape=jax.ShapeDtypeStruct((n_seq, lq, heads * LANE), F32),
        compiler_params=_cparams("arbitrary"),
        name="sample_attention",
    )(page_table, wt, qlat, qg, c_new, kr_new_t, cs_t, cs_new_t, wuv, cache_kv, cache_kr_t)


def _ln_swish(yc, g, b):
    mu = jnp.mean(yc, axis=-1, keepdims=True)
    xc = yc - mu
    y = xc * lax.rsqrt(jnp.mean(xc * xc, axis=-1, keepdims=True) + LN_EPS) * g + b
    return _silu(y)


def _conv_prompt_body(a_ref, gate_ref, w_ref, b_ref, lg_ref, lb_ref, act_ref, new_ref, xp_ref, slab_ref, *,
                      taps, chunk):
    seq = a_ref.shape[0]
    front = xp_ref.shape[0] - seq
    shift = front - (taps - 1)
    xp_ref[0:front, :] = jnp.zeros((front, xp_ref.shape[1]), F32)
    xp_ref[front:, :] = a_ref[...] * _sigmoid(gate_ref[...])
    new_ref[...] = xp_ref[seq + shift:seq + front, :]

    def body(i, carry):
        r0 = pl.multiple_of(i * chunk, chunk)
        slab_ref[...] = xp_ref[pl.ds(r0, chunk + front), :]
        acc = jnp.zeros((chunk, xp_ref.shape[1]), F32) + b_ref[...]
        for j in range(taps):
            acc = acc + slab_ref[shift + j:shift + j + chunk, :] * w_ref[j:j + 1, :]
        act_ref[pl.ds(r0, chunk), :] = _ln_swish(acc, lg_ref[...], lb_ref[...]).astype(BF16)
        return carry

    lax.fori_loop(0, seq // chunk, body, 0)


def _conv_prompt(z, cols, wts, layer, batch, seq, taps):
    ch = cols["glu_a"][1]
    a_off = cols["glu_a"][0] // ch
    g_off = cols["glu_g"][0] // ch
    front = -(-(taps - 1) // SUBLANE) * SUBLANE

    def wspec(a):
        return pl.BlockSpec((None,) + a.shape[1:], lambda b: (layer,) + (0,) * (a.ndim - 1))

    chunk = _tile(seq, 64)
    return pl.pallas_call(
        functools.partial(_conv_prompt_body, taps=taps, chunk=chunk),
        grid=(batch,),
        in_specs=[pl.BlockSpec((seq, ch), lambda b: (b, a_off)), pl.BlockSpec((seq, ch), lambda b: (b, g_off)),
                  wspec(wts["conv_w"]), wspec(wts["conv_b"]), wspec(wts["ln_g"]), wspec(wts["ln_b"])],
        out_specs=[pl.BlockSpec((seq, ch), lambda b: (b, 0)),
                   pl.BlockSpec((None, taps - 1, ch), lambda b: (b, 0, 0))],
        out_shape=[jax.ShapeDtypeStruct((batch * seq, ch), BF16),
                   jax.ShapeDtypeStruct((batch, taps - 1, ch), F32)],
        scratch_shapes=[pltpu.VMEM((front + seq, ch), F32), pltpu.VMEM((front + chunk, ch), F32)],
        compiler_params=_cparams("parallel"),
        name="conv_prompt",
    )(z, z, wts["conv_w"], wts["conv_b"], wts["ln_g"], wts["ln_b"])


def _conv_sample_body(a_ref, gate_ref, st_ref, w_ref, b_ref, lg_ref, lb_ref, act_ref, new_ref, xp_ref,
                      u_ref, yc_ref, *, taps, lq, nseq):
    hist = taps - 1
    u_ref[...] = a_ref[...] * _sigmoid(gate_ref[...])
    w = w_ref[...]
    for s in range(nseq):
        xp_ref[0:hist, :] = st_ref[s]
        xp_ref[hist:hist + lq, :] = u_ref[s * lq:(s + 1) * lq, :]
        new_ref[s] = xp_ref[lq:lq + hist, :]
        for t in range(lq):
            yc_ref[s * lq + t:s * lq + t + 1, :] = jnp.sum(xp_ref[t:t + taps, :] * w, axis=0, keepdims=True)
    act_ref[...] = _ln_swish(yc_ref[...] + b_ref[...], lg_ref[...], lb_ref[...]).astype(BF16)


def _conv_sample(z, state, cols, wts, layer, lq, taps, nseq):
    ch = cols["glu_a"][1]
    a_off = cols["glu_a"][0] // ch
    g_off = cols["glu_g"][0] // ch
    n = state.shape[1]
    rows = nseq * lq

    def wspec(a):
        return pl.BlockSpec((None,) + a.shape[1:], lambda b: (layer,) + (0,) * (a.ndim - 1))

    return pl.pallas_call(
        functools.partial(_conv_sample_body, taps=taps, lq=lq, nseq=nseq),
        grid=(n // nseq,),
        in_specs=[pl.BlockSpec((rows, ch), lambda b: (b, a_off)), pl.BlockSpec((rows, ch), lambda b: (b, g_off)),
                  pl.BlockSpec((None, nseq, taps - 1, ch), lambda b: (layer, b, 0, 0)),
                  wspec(wts["conv_w"]), wspec(wts["conv_b"]), wspec(wts["ln_g"]), wspec(wts["ln_b"])],
        out_specs=[pl.BlockSpec((rows, ch), lambda b: (b, 0)),
                   pl.BlockSpec((nseq, taps - 1, ch), lambda b: (b, 0, 0))],
        out_shape=[jax.ShapeDtypeStruct((n * lq, ch), BF16),
                   jax.ShapeDtypeStruct((n, taps - 1, ch), F32)],
        scratch_shapes=[pltpu.VMEM((-(-(taps - 1 + lq) // SUBLANE) * SUBLANE, ch), F32),
                        pltpu.VMEM((rows, ch), F32), pltpu.VMEM((rows, ch), F32)],
        compiler_params=_cparams("parallel"),
        name="conv_sample",
    )(z, z, state, wts["conv_w"], wts["conv_b"], wts["ln_g"], wts["ln_b"])


def _log_sigmoid(x):
    return jnp.minimum(x, 0.0) - jnp.log1p(jnp.exp(-jnp.abs(x)))


def _split3(x):
    h1 = x.astype(BF16)
    r1 = x - h1.astype(F32)
    h2 = r1.astype(BF16)
    h3 = (r1 - h2.astype(F32)).astype(BF16)
    return h1, h2, h3


def _gla_body(gq_ref, gk_ref, gv_ref, ga_ref, gr_ref, s0_ref, wa_ref, ba_ref, gn_ref, y_ref, sout_ref,
              st_ref, oi_ref, *, heads, dk, dv, valid, carried):
    T = GLA_TILE
    R = GLA_BLOCK
    nb = T // R
    pairs = heads // 2
    hk = heads * dk
    ti = pl.program_id(1)

    if carried:
        @pl.when(ti == 0)
        def _():
            st_ref[...] = jnp.zeros(st_ref.shape, F32)

    row = lax.broadcasted_iota(jnp.int32, (T, T), 0)
    col = lax.broadcasted_iota(jnp.int32, (T, T), 1)
    same = (row // R) == (col // R)
    tri = jnp.where(same & (col <= row), 1.0, 0.0).astype(BF16)
    ones_blk = jnp.where(same, 1.0, 0.0).astype(BF16)

    la = _log_sigmoid(_dot(ga_ref[...].astype(BF16), wa_ref[...]) + ba_ref[...]) * (1.0 / GLA_TAU)
    if valid < R:
        rr = lax.broadcasted_iota(jnp.int32, (T, hk), 0) % R
        la = jnp.where(rr < valid, la, 0.0)
    parts = jnp.concatenate(_split3(la), axis=-1)
    cums = _dot(jnp.concatenate([tri, ones_blk], axis=0), parts)
    bc = cums[:T, :hk] + cums[:T, hk:2 * hk] + cums[:T, 2 * hk:]
    bl = cums[T:, :hk] + cums[T:, hk:2 * hk] + cums[T:, 2 * hk:]
    q = gq_ref[...] * (dk ** -0.5)
    k = gk_ref[...]
    v = gv_ref[...]
    qe = q * jnp.exp(bc)
    kd = k * jnp.exp(bl - bc)
    dec = jnp.exp(bl)

    hd_row = lax.broadcasted_iota(jnp.int32, (hk, heads * dv), 0) // dk
    hd_col = lax.broadcasted_iota(jnp.int32, (hk, heads * dv), 1) // dv
    expand = jnp.where(hd_row == hd_col, 1.0, 0.0).astype(BF16)
    jrow = lax.broadcasted_iota(jnp.int32, (R, hk), 0)
    for t in range(nb):
        rs = slice(t * R, (t + 1) * R)
        b, qb, kb, vb = bc[rs], q[rs], k[rs], v[rs]
        p = []
        for i in range(R):
            e = jnp.exp(jnp.where(jrow <= i, b[i:i + 1, :] - b, -jnp.inf))
            p.append((e * qb[i:i + 1, :] * kb).astype(BF16))
        a = _dot(jnp.concatenate(p, axis=0), expand)
        for i in range(R):
            oi_ref[t * R + i:t * R + i + 1, :] = jnp.sum(a[i * R:(i + 1) * R] * vb, axis=0, keepdims=True)
    o = oi_ref[...]

    lane = lax.broadcasted_iota(jnp.int32, (T, 2 * dk), 1)
    blk = lax.broadcasted_iota(jnp.int32, (T, 2 * dk), 0) // R
    lane_blk = lax.broadcasted_iota(jnp.int32, (R, 2 * dk), 1)
    o_cols = []
    for pr in range(pairs):
        ls = slice(pr * 2 * dk, (pr + 1) * 2 * dk)
        kd_p, qe_p, dec_p = kd[:, ls], qe[:, ls], dec[:, ls]
        kv = None
        for hh in range(2):
            h = 2 * pr + hh
            kd_h = jnp.where((lane // dk) == hh, kd_p, 0.0)
            stack = jnp.concatenate([jnp.where(blk == t, kd_h, 0.0) for t in range(nb)], axis=-1).astype(BF16)
            vt = v[:, h * dv:(h + 1) * dv].T.astype(BF16)
            part = _dot(vt, stack)
            kv = part if kv is None else kv + part
        o_pair = [[], []]
        for t in range(nb):
            rs = slice(t * R, (t + 1) * R)
            if carried:
                st = st_ref[pr]
            else:
                s0 = s0_ref[t]
                st = jnp.concatenate([s0[2 * pr], s0[2 * pr + 1]], axis=0).T
            stb = st.astype(BF16)
            for hh in range(2):
                qm = jnp.where((lane_blk // dk) == hh, qe_p[rs], 0.0).astype(BF16)
                o_pair[hh].append(_dot_nt(qm, stb))
            st = st * dec_p[t * R:t * R + 1, :] + kv[:, t * 2 * dk:(t + 1) * 2 * dk]
            if carried:
                st_ref[pr] = st
            else:
                stt = st.T
                sout_ref[t, 2 * pr] = stt[:dk]
                sout_ref[t, 2 * pr + 1] = stt[dk:]
        for hh in range(2):
            o_cols.append(jnp.concatenate(o_pair[hh], axis=0))
    o = o + jnp.concatenate(o_cols, axis=-1)

    gr = gr_ref[...]
    for h in range(heads):
        cs = slice(h * dv, (h + 1) * dv)
        y_ref[:, cs] = (_rms(o[:, cs], gn_ref[...]) * _silu(gr[:, cs])).astype(BF16)

    if carried:
        @pl.when(ti == pl.num_programs(1) - 1)
        def _():
            for pr in range(pairs):
                stt = st_ref[pr].T
                sout_ref[2 * pr] = stt[:dk]
                sout_ref[2 * pr + 1] = stt[dk:]


def _gla(zg, cols, s0, wts, layer, batch, seq, heads, dk, dv, valid, carried):
    T = GLA_TILE
    m = zg.shape[0]
    hk, hv = heads * dk, heads * dv
    if carried:
        grid = (batch, seq // T)
        rowblk = lambda b, t: b * (seq // T) + t
        s_spec = pl.BlockSpec((None, heads, dk, dv), lambda b, t: (b, 0, 0, 0))
        s0_arr = jnp.zeros((1, heads, dk, dv), F32)
        s0_spec = pl.BlockSpec((1, heads, dk, dv), lambda b, t: (0, 0, 0, 0))
        n_state = batch
    else:
        nb = T // GLA_BLOCK
        grid = (1, m // T)
        rowblk = lambda b, t: t
        s_spec = pl.BlockSpec((nb, heads, dk, dv), lambda b, t: (t, 0, 0, 0))
        s0_arr = s0
        s0_spec = pl.BlockSpec((None, nb, heads, dk, dv), lambda b, t: (layer, t, 0, 0, 0))
        n_state = m // GLA_BLOCK

    def zspec(name):
        off, w = cols[name]
        return pl.BlockSpec((T, w), lambda b, t: (rowblk(b, t), off // w))

    def wspec(a):
        return pl.BlockSpec((None,) + a.shape[1:], lambda b, t: (layer,) + (0,) * (a.ndim - 1))

    return pl.pallas_call(
        functools.partial(_gla_body, heads=heads, dk=dk, dv=dv, valid=valid, carried=carried),
        grid=grid,
        in_specs=[zspec("gq"), zspec("gk"), zspec("gv"), zspec("ga"), zspec("gr"), s0_spec,
                  wspec(wts["gla_wa"]), wspec(wts["gla_ba"]), wspec(wts["gla_norm"])],
        out_specs=[pl.BlockSpec((T, hv), lambda b, t: (rowblk(b, t), 0)), s_spec],
        out_shape=[jax.ShapeDtypeStruct((m, hv), BF16), jax.ShapeDtypeStruct((n_state, heads, dk, dv), F32)],
        scratch_shapes=[pltpu.VMEM((heads // 2, dv, 2 * dk), F32), pltpu.VMEM((T, hv), F32)],
        compiler_params=_cparams("parallel", "arbitrary"),
        name="gla_carried" if carried else "gla_blocks",
    )(zg, zg, zg, zg, zg, s0_arr, wts["gla_wa"], wts["gla_ba"], wts["gla_norm"])


def _merge_body(ya_ref, yb_ref, yc_ref, g0_ref, g1_ref, g2_ref, wpa_ref, wpb_ref, wpc_ref, m_ref):
    m = _sigmoid(g0_ref[...]) * _dot(ya_ref[...], wpa_ref[...])
    m = m + _sigmoid(g1_ref[...]) * _dot(yb_ref[...], wpb_ref[...])
    m = m + _sigmoid(g2_ref[...]) * _dot(yc_ref[...], wpc_ref[...])
    m_ref[...] = m.astype(BF16)


def _merge(ya, yb, yc, z, cols, wts, layer, tm):
    m = ya.shape[0]
    d = wts["w_pa"].shape[-1]
    mg_off = cols["mg"][0] // d

    def wspec(a):
        return pl.BlockSpec((None,) + a.shape[1:], lambda i: (layer,) + (0,) * (a.ndim - 1))

    def gspec(idx):
        return pl.BlockSpec((tm, d), lambda i: (i, mg_off + idx))

    def yspec(a):
        return pl.BlockSpec((tm, a.shape[1]), lambda i: (i, 0))

    return pl.pallas_call(
        _merge_body,
        grid=(m // tm,),
        in_specs=[yspec(ya), yspec(yb), yspec(yc), gspec(0), gspec(1), gspec(2),
                  wspec(wts["w_pa"]), wspec(wts["w_pb"]), wspec(wts["w_pc"])],
        out_specs=pl.BlockSpec((tm, d), lambda i: (i, 0)),
        out_shape=jax.ShapeDtypeStruct((m, d), BF16),
        compiler_params=_cparams("parallel"),
        name="merge",
    )(ya, yb, yc, z, z, z, wts["w_pa"], wts["w_pb"], wts["w_pc"])


def _out_proj_body(x_ref, m_ref, w_ref, o_ref):
    o_ref[...] = x_ref[...] + _dot(m_ref[...], w_ref[...])


def _out_proj(x, mm, w, layer, tm, tn):
    m, d = x.shape
    return pl.pallas_call(
        _out_proj_body,
        grid=(m // tm, d // tn),
        in_specs=[pl.BlockSpec((tm, tn), lambda i, j: (i, j)),
                  pl.BlockSpec((tm, d), lambda i, j: (i, 0)),
                  pl.BlockSpec((None, d, tn), lambda i, j: (layer, 0, j))],
        out_specs=pl.BlockSpec((tm, tn), lambda i, j: (i, j)),
        out_shape=jax.ShapeDtypeStruct((m, d), F32),
        compiler_params=_cparams("parallel", "parallel"),
        name="out_proj",
    )(x, mm, w)


def _ffn_body(x_ref, g_ref, wg_ref, wu_ref, wo_ref, o_ref, h_ref, acc_ref):
    f = pl.program_id(1)

    @pl.when(f == 0)
    def _():
        h_ref[...] = _rms(x_ref[...], g_ref[...]).astype(BF16)
        acc_ref[...] = jnp.zeros(acc_ref.shape, F32)

    h = h_ref[...]
    act = (_silu(_dot(h, wg_ref[...])) * _dot(h, wu_ref[...])).astype(BF16)
    acc_ref[...] += _dot(act, wo_ref[...])

    @pl.when(f == pl.num_programs(1) - 1)
    def _():
        o_ref[...] = x_ref[...] + acc_ref[...]


def _ffn(x, g, w_in, w_out, layer, tm, tf):
    m, d = x.shape
    dff = w_out.shape[1]
    nf = dff // tf
    return pl.pallas_call(
        _ffn_body,
        grid=(m // tm, nf),
        in_specs=[pl.BlockSpec((tm, d), lambda i, f: (i, 0)),
                  pl.BlockSpec((None, 1, d), lambda i, f: (layer, 0, 0)),
                  pl.BlockSpec((None, d, tf), lambda i, f: (layer, 0, f)),
                  pl.BlockSpec((None, d, tf), lambda i, f: (layer, 0, nf + f)),
                  pl.BlockSpec((None, tf, d), lambda i, f: (layer, f, 0))],
        out_specs=pl.BlockSpec((tm, d), lambda i, f: (i, 0)),
        out_shape=jax.ShapeDtypeStruct((m, d), F32),
        scratch_shapes=[pltpu.VMEM((tm, d), BF16), pltpu.VMEM((tm, d), F32)],
        compiler_params=_cparams("parallel", "arbitrary"),
        name="ffn",
    )(x, g, w_in, w_in, w_out)


def _rope_angles(pos):
    half = 32
    inv = jnp.exp(-math.log(ROPE_BASE) * jnp.arange(half, dtype=F32) / half)
    ang = pos.astype(F32)[:, None] * inv[None, :]
    return jnp.cos(ang), jnp.sin(ang)


def _rope_tables(pos):
    cos, sin = _rope_angles(pos)
    z = jnp.zeros((pos.shape[0], LANE // 2), F32)
    return jnp.concatenate([cos, cos, z], -1), jnp.concatenate([-sin, sin, z], -1)


def _pad_cols(a, width):
    return jnp.pad(a, [(0, 0)] * (a.ndim - 1) + [(0, width - a.shape[-1])])


def kernel(x_prompt, x_sample, cache_kv, cache_krope, state_conv, state_gla, page_table, attn_norm, w_in, cq_norm, w_uq, q_norm, kv_norm, w_uk, k_norm, w_uv, w_pa, conv_w, conv_b, conv_ln_g, conv_ln_b, w_pb, gla_wa, gla_ba, gla_norm, w_pc, w_o, ffn_norm, w_ffn_in, w_ffn_out):
    depth, d_model, _ = w_in.shape
    batch, seq, _ = x_prompt.shape
    n_seq, lq, _ = x_sample.shape
    q_lora = cq_norm.shape[1]
    kv_lora = kv_norm.shape[1]
    heads, qk_head = w_uq.shape[2], w_uq.shape[3]
    qk_nope = w_uk.shape[3]
    qk_rope = qk_head - qk_nope
    conv_ch = conv_b.shape[1]
    taps = conv_w.shape[1]
    g_heads, g_dk, g_dv = state_gla.shape[2], state_gla.shape[3], state_gla.shape[4]
    g_rank = gla_wa.shape[1]
    n_pages, page = page_table.shape[1], cache_kv.shape[2]
    past = n_pages * page
    assert (qk_nope, qk_rope, kv_lora) == (LANE, LANE // 2, 2 * LANE) and g_heads % 2 == 0 and 2 * g_dk == LANE

    ref_sizes = dict(cq=q_lora, ckv=kv_lora, kr=qk_rope, glu=2 * conv_ch, gq=g_heads * g_dk, gk=g_heads * g_dk,
                     gv=g_heads * g_dv, ga=g_rank, gr=g_heads * g_dv, mg=3 * d_model)
    ref_off, off = {}, 0
    for name, size in ref_sizes.items():
        ref_off[name] = off
        off += size
    pieces = [("mg", ref_off["mg"], 3 * d_model, 3 * d_model),
              ("glu_a", ref_off["glu"], conv_ch, conv_ch), ("glu_g", ref_off["glu"] + conv_ch, conv_ch, conv_ch),
              ("cq", ref_off["cq"], q_lora, q_lora), ("gv", ref_off["gv"], g_heads * g_dv, g_heads * g_dv),
              ("gr", ref_off["gr"], g_heads * g_dv, g_heads * g_dv), ("ckv", ref_off["ckv"], kv_lora, kv_lora),
              ("gq", ref_off["gq"], g_heads * g_dk, g_heads * g_dk), ("gk", ref_off["gk"], g_heads * g_dk, g_heads * g_dk),
              ("kr", ref_off["kr"], qk_rope, LANE), ("ga", ref_off["ga"], g_rank, LANE)]
    cols, off, packed = {}, 0, []
    for name, src, size, width in pieces:
        cols[name] = (off, width)
        packed.append(_pad_cols(w_in[:, :, src:src + size], width))
        off += width
    cols["mg"] = (0, d_model)
    w_in_p = jnp.concatenate(packed, axis=-1).astype(BF16)
    n_in = w_in_p.shape[-1]

    wuq_p = jnp.concatenate([w_uq[..., :qk_nope], _pad_cols(w_uq[..., qk_nope:], LANE)], axis=-1)
    wuq_p = wuq_p.reshape(depth, q_lora, heads * HEAD_PAD).astype(BF16)
    pad_gain = lambda g: jnp.concatenate([g[:, :qk_nope], _pad_cols(g[:, qk_nope:], LANE)], -1)[:, None, :]
    half = qk_rope // 2
    g_r1, g_r2 = k_norm[:, qk_nope:qk_nope + half], k_norm[:, qk_nope + half:]
    zeros_h = jnp.zeros_like(g_r1)
    attn_w = dict(
        g_cq=cq_norm[:, None, :], w_uq=wuq_p, g_q=pad_gain(q_norm), g_kv=kv_norm[:, None, :],
        w_uk=w_uk.reshape(depth, kv_lora, heads * qk_nope).astype(BF16), g_k=pad_gain(k_norm),
        w_uk3=jnp.transpose(w_uk, (0, 2, 1, 3)).astype(BF16),
        g_t1=jnp.concatenate([g_r1, g_r2, zeros_h, zeros_h], -1)[:, None, :],
        g_t2=jnp.concatenate([zeros_h, zeros_h, g_r1, -g_r2], -1)[:, None, :],
    )
    w_ukT = jnp.transpose(w_uk.reshape(depth, kv_lora, heads * qk_nope), (0, 2, 1)).astype(BF16)
    w_uv_heads = jnp.transpose(w_uv, (0, 2, 1, 3)).astype(BF16)
    w_uv_flat = w_uv.reshape(depth, kv_lora, heads * w_uv.shape[3]).astype(BF16)
    conv_wts = dict(conv_w=conv_w, conv_b=conv_b[:, None, :], ln_g=conv_ln_g[:, None, :], ln_b=conv_ln_b[:, None, :])
    gla_wts = dict(gla_wa=jnp.pad(gla_wa, ((0, 0), (0, LANE - g_rank), (0, 0))).astype(BF16),
                   gla_ba=gla_ba[:, None, :], gla_norm=gla_norm[:, None, :])
    merge_wts = dict(w_pa=w_pa.astype(BF16), w_pb=w_pb.astype(BF16), w_pc=w_pc.astype(BF16))
    w_o_b = w_o.astype(BF16)
    w_ffn_in_b = w_ffn_in.astype(BF16)
    w_ffn_out_b = w_ffn_out.astype(BF16)
    g_attn = attn_norm[:, None, :]
    g_ffn = ffn_norm[:, None, :]

    cos_p, sin_p = _rope_tables(jnp.arange(seq, dtype=jnp.int32))
    cos_s, sin_s = _rope_tables(past + jnp.arange(lq, dtype=jnp.int32))
    cos_s, sin_s = jnp.tile(cos_s, (n_seq, 1)), jnp.tile(sin_s, (n_seq, 1))
    tk_s = _tile(past, 1024)
    kc, ks = _rope_angles(jnp.arange(past + LANE, dtype=jnp.int32))
    key_cs_t = jnp.concatenate([kc, kc, ks, ks], axis=-1).T
    cs_past_t = jnp.transpose(key_cs_t[:, :past].reshape(2 * qk_rope, past // tk_s, tk_s), (1, 0, 2))
    cs_new_t = key_cs_t[:, past:]
    cache_kr_t = jnp.swapaxes(cache_krope, 2, 3)

    mp, ms = batch * seq, n_seq * lq
    tm_p = _tile(mp, 1024)
    tn_in = _tile(n_in, 512)
    tf = _tile(w_ffn_out.shape[1], 512)
    gla_cols = {k: cols[k] for k in ("gq", "gk", "gv", "ga", "gr")}
    gla_lo = min(v[0] for v in gla_cols.values())

    xp = x_prompt.reshape(mp, d_model)
    xs = x_sample.reshape(ms, d_model)
    outs = {k: [] for k in ("ckv_p", "kr_p", "conv_p", "gla_p", "ckv_s", "kr_s", "conv_s", "gla_s")}
    kr_off = cols["kr"][0]
    for l in range(depth):
        z = _norm_matmul(xp, g_attn, w_in_p, l, tm_p, tn_in)
        q, k, ckv, cb_t = _qk_prompt(z, cols, cos_p, sin_p, attn_w, l, _tile(seq, 256), heads, qk_head)
        ya = _prompt_attention(q, k, cb_t, w_uv_heads, l, batch, seq, heads, _tile(seq, 512), _tile(seq, 512))
        yb, conv_new = _conv_prompt(z, cols, conv_wts, l, batch, seq, taps)
        yc, gla_new = _gla(z, gla_cols, None, gla_wts, l, batch, seq, g_heads, g_dk, g_dv, GLA_BLOCK, True)
        mm = _merge(ya, yb, yc, z, cols, merge_wts, l, _tile(mp, 256))
        x1 = _out_proj(xp, mm, w_o_b, l, tm_p, _tile(d_model, 512))
        xp = _ffn(x1, g_ffn, w_ffn_in_b, w_ffn_out_b, l, _tile(mp, 512), tf)
        outs["ckv_p"].append(ckv.reshape(batch, seq, kv_lora))
        outs["kr_p"].append(z[:, kr_off:kr_off + qk_rope].reshape(batch, seq, qk_rope))
        outs["conv_p"].append(conv_new)
        outs["gla_p"].append(gla_new)

        z = _norm_matmul(xs, g_attn, w_in_p, l, ms, tn_in)
        qlat, qg, ckv = _q_sample(z, cols, cos_s, sin_s, attn_w, l, heads, qk_head)
        kr_new = z[:, kr_off:kr_off + qk_rope]
        rows_new = -(-lq // SUBLANE) * SUBLANE
        c_new = jnp.pad(ckv.reshape(n_seq, lq, kv_lora), ((0, 0), (0, rows_new - lq), (0, 0)))
        kr_new_t = jnp.pad(jnp.swapaxes(kr_new.reshape(n_seq, lq, qk_rope), 1, 2), ((0, 0), (0, 0), (0, LANE - lq)))
        ya = _sample_attention(page_table, w_ukT, qlat.reshape(n_seq, lq * heads, kv_lora),
                               qg.reshape(n_seq, lq * heads, 2 * LANE), c_new, kr_new_t, cs_past_t, cs_new_t,
                               w_uv_flat, cache_kv, cache_kr_t, l, heads, qk_head, lq, tk_s)
        ya = ya.reshape(ms, heads * LANE).astype(BF16)
        yb, conv_new = _conv_sample(z, state_conv, cols, conv_wts, l, lq, taps, SUBLANE)
        zg = z[:, gla_lo:]
        zg = jnp.pad(zg.reshape(n_seq, lq, -1), ((0, 0), (0, GLA_BLOCK - lq), (0, 0))).reshape(n_seq * GLA_BLOCK, -1)
        zg_cols = {k: (v[0] - gla_lo, v[1]) for k, v in gla_cols.items()}
        yc, gla_new = _gla(zg, zg_cols, state_gla, gla_wts, l, 1, GLA_BLOCK, g_heads, g_dk, g_dv, lq, False)
        yc = yc.reshape(n_seq, GLA_BLOCK, -1)[:, :lq].reshape(ms, -1)
        mm = _merge(ya, yb, yc, z, cols, merge_wts, l, _tile(ms, 256))
        x1 = _out_proj(xs, mm, w_o_b, l, ms, _tile(d_model, 512))
        xs = _ffn(x1, g_ffn, w_ffn_in_b, w_ffn_out_b, l, ms, tf)
        outs["ckv_s"].append(ckv.reshape(n_seq, lq, kv_lora))
        outs["kr_s"].append(kr_new.reshape(n_seq, lq, qk_rope))
        outs["conv_s"].append(conv_new)
        outs["gla_s"].append(gla_new)

    stack = lambda key: jnp.stack(outs[key])
    return (xp.reshape(batch, seq, d_model), xs.reshape(n_seq, lq, d_model),
            stack("ckv_p"), stack("kr_p"), stack("conv_p"), stack("gla_p"),
            stack("ckv_s"), stack("kr_s"), stack("conv_s"), stack("gla_s"))
```

```python
import functools
import math

import jax
import jax.numpy as jnp
from jax import lax
from jax.experimental import pallas as pl
from jax.experimental.pallas import tpu as pltpu

F32 = jnp.float32
BF16 = jnp.bfloat16

NORM_EPS = 1e-6
LN_EPS = 1e-5
ROPE_BASE = 10000.0
GLA_TAU = 16.0

LANE = 128
SUBLANE = 8
VMEM_LIMIT = 56 * 1024 * 1024
GLA_BLOCK = 16
GLA_TILE = 128
HEAD_PAD = 256


def _cparams(*sem):
    return pltpu.CompilerParams(dimension_semantics=sem, vmem_limit_bytes=VMEM_LIMIT)


def _tile(n, pref):
    if n <= pref:
        return n
    t = pref
    while n % t:
        t -= SUBLANE
    return t


def _rms(x, g, eps=NORM_EPS):
    return x * lax.rsqrt(jnp.mean(x * x, axis=-1, keepdims=True) + eps) * g


def _dot(a, b):
    return jnp.dot(a, b, preferred_element_type=F32)


def _dot_nt(a, b):
    return lax.dot_general(a, b, (((1,), (1,)), ((), ())), preferred_element_type=F32)


def _sigmoid(x):
    return 1.0 / (1.0 + jnp.exp(-x))


def _silu(x):
    return x * _sigmoid(x)


def _rope_lanes(x, cos_t, sin_t):
    return x * cos_t + (pltpu.roll(x, 96, 1) + pltpu.roll(x, 32, 1)) * sin_t


def _norm_matmul_body(x_ref, g_ref, w_ref, o_ref, h_ref):
    @pl.when(pl.program_id(1) == 0)
    def _():
        h_ref[...] = _rms(x_ref[...], g_ref[...]).astype(BF16)

    o_ref[...] = _dot(h_ref[...], w_ref[...])


def _norm_matmul(x, g, w, layer, tm, tn):
    m, k = x.shape
    n = w.shape[-1]
    return pl.pallas_call(
        _norm_matmul_body,
        grid=(m // tm, n // tn),
        in_specs=[
            pl.BlockSpec((tm, k), lambda i, j: (i, 0)),
            pl.BlockSpec((None, 1, k), lambda i, j: (layer, 0, 0)),
            pl.BlockSpec((None, k, tn), lambda i, j: (layer, 0, j)),
        ],
        out_specs=pl.BlockSpec((tm, tn), lambda i, j: (i, j)),
        out_shape=jax.ShapeDtypeStruct((m, n), F32),
        scratch_shapes=[pltpu.VMEM((tm, k), BF16)],
        compiler_params=_cparams("parallel", "arbitrary"),
        name="in_proj",
    )(x, g, w)


def _head_norm_q(qraw, h, gq_ref, cos_t, sin_t, width):
    lo = h * HEAD_PAD
    qn = qraw[:, lo:lo + LANE]
    qr = qraw[:, lo + LANE:lo + HEAD_PAD]
    ss = jnp.sum(qn * qn, axis=-1, keepdims=True) + jnp.sum(qr * qr, axis=-1, keepdims=True)
    rinv = lax.rsqrt(ss * (1.0 / width) + NORM_EPS)
    qn = qn * rinv * gq_ref[:, :LANE]
    qr = _rope_lanes(qr * rinv * gq_ref[:, LANE:], cos_t, sin_t)
    return qn, qr


def _qk_prompt_body(zcq_ref, zckv_ref, zkr_ref, cos_ref, sin_ref, gcq_ref, wuq_ref, gq_ref, gkv_ref,
                    wuk_ref, gk_ref, q_ref, k_ref, ckv_ref, cbt_ref, *, heads, width):
    cos_t = cos_ref[...]
    sin_t = sin_ref[...]
    cq = _rms(zcq_ref[...], gcq_ref[...]).astype(BF16)
    qraw = _dot(cq, wuq_ref[...])
    c = _rms(zckv_ref[...], gkv_ref[...])
    ckv_ref[...] = c
    cbt_ref[...] = c.T.astype(BF16)
    kn = _dot(c.astype(BF16), wuk_ref[...])
    krp = zkr_ref[...]
    kr_ss = jnp.sum(krp * krp, axis=-1, keepdims=True)
    scale = width ** -0.5
    for h in range(heads):
        qn, qr = _head_norm_q(qraw, h, gq_ref, cos_t, sin_t, width)
        lo = h * HEAD_PAD
        q_ref[:, lo:lo + LANE] = (qn * scale).astype(BF16)
        q_ref[:, lo + LANE:lo + HEAD_PAD] = (qr * scale).astype(BF16)
        kh = kn[:, h * LANE:(h + 1) * LANE]
        rk = lax.rsqrt((jnp.sum(kh * kh, axis=-1, keepdims=True) + kr_ss) * (1.0 / width) + NORM_EPS)
        k_ref[:, lo:lo + LANE] = (kh * rk * gk_ref[:, :LANE]).astype(BF16)
        k_ref[:, lo + LANE:lo + HEAD_PAD] = _rope_lanes(krp * rk * gk_ref[:, LANE:], cos_t, sin_t).astype(BF16)


def _qk_prompt(z, cols, cos_t, sin_t, wts, layer, tm, heads, width):
    m = z.shape[0]
    n_tab = cos_t.shape[0] // tm
    hp = heads * HEAD_PAD

    def zspec(name):
        off, w = cols[name]
        return pl.BlockSpec((tm, w), lambda i: (i, off // w))

    def wspec(a):
        return pl.BlockSpec((None,) + a.shape[1:], lambda i: (layer,) + (0,) * (a.ndim - 1))

    tab = pl.BlockSpec((tm, LANE), lambda i: (i % n_tab, 0))
    return pl.pallas_call(
        functools.partial(_qk_prompt_body, heads=heads, width=width),
        grid=(m // tm,),
        in_specs=[zspec("cq"), zspec("ckv"), zspec("kr"), tab, tab,
                  wspec(wts["g_cq"]), wspec(wts["w_uq"]), wspec(wts["g_q"]), wspec(wts["g_kv"]),
                  wspec(wts["w_uk"]), wspec(wts["g_k"])],
        out_specs=[pl.BlockSpec((tm, hp), lambda i: (i, 0)), pl.BlockSpec((tm, hp), lambda i: (i, 0)),
                   pl.BlockSpec((tm, cols["ckv"][1]), lambda i: (i, 0)),
                   pl.BlockSpec((cols["ckv"][1], tm), lambda i: (0, i))],
        out_shape=[jax.ShapeDtypeStruct((m, hp), BF16), jax.ShapeDtypeStruct((m, hp), BF16),
                   jax.ShapeDtypeStruct((m, cols["ckv"][1]), F32),
                   jax.ShapeDtypeStruct((cols["ckv"][1], m), BF16)],
        compiler_params=_cparams("parallel"),
        name="qk_prompt",
    )(z, z, z, cos_t, sin_t, wts["g_cq"], wts["w_uq"], wts["g_q"], wts["g_kv"], wts["w_uk"], wts["g_k"])


def _q_sample_body(zcq_ref, zckv_ref, cos_ref, sin_ref, gcq_ref, wuq_ref, gq_ref, gkv_ref, wuk3_ref,
                   gk_ref, gt1_ref, gt2_ref, qlat_ref, qg_ref, ckv_ref, *, heads, width):
    cos_t = cos_ref[...]
    sin_t = sin_ref[...]
    cq = _rms(zcq_ref[...], gcq_ref[...]).astype(BF16)
    qraw = _dot(cq, wuq_ref[...])
    ckv_ref[...] = _rms(zckv_ref[...], gkv_ref[...])
    zero = jnp.zeros((qraw.shape[0], LANE), BF16)
    for h in range(heads):
        qn, qr = _head_norm_q(qraw, h, gq_ref, cos_t, sin_t, width)
        lo = h * HEAD_PAD
        qlat_ref[:, lo:lo + HEAD_PAD] = _dot_nt((qn * gk_ref[:, :LANE]).astype(BF16), wuk3_ref[h]).astype(BF16)
        feat = qr * gt1_ref[...] + (pltpu.roll(qr, 32, 1) + pltpu.roll(qr, 96, 1)) * gt2_ref[...]
        qg_ref[:, lo:lo + LANE] = feat.astype(BF16)
        qg_ref[:, lo + LANE:lo + HEAD_PAD] = zero


def _q_sample(z, cols, cos_t, sin_t, wts, layer, heads, width):
    m = z.shape[0]
    tm = m
    hp = heads * HEAD_PAD

    def zspec(name):
        off, w = cols[name]
        return pl.BlockSpec((tm, w), lambda i: (i, off // w))

    def wspec(a):
        return pl.BlockSpec((None,) + a.shape[1:], lambda i: (layer,) + (0,) * (a.ndim - 1))

    tab = pl.BlockSpec((tm, LANE), lambda i: (i, 0))
    kvw = cols["ckv"][1]
    return pl.pallas_call(
        functools.partial(_q_sample_body, heads=heads, width=width),
        grid=(m // tm,),
        in_specs=[zspec("cq"), zspec("ckv"), tab, tab,
                  wspec(wts["g_cq"]), wspec(wts["w_uq"]), wspec(wts["g_q"]), wspec(wts["g_kv"]),
                  wspec(wts["w_uk3"]), wspec(wts["g_k"]), wspec(wts["g_t1"]), wspec(wts["g_t2"])],
        out_specs=[pl.BlockSpec((tm, hp), lambda i: (i, 0)), pl.BlockSpec((tm, hp), lambda i: (i, 0)),
                   pl.BlockSpec((tm, kvw), lambda i: (i, 0))],
        out_shape=[jax.ShapeDtypeStruct((m, hp), BF16), jax.ShapeDtypeStruct((m, hp), BF16),
                   jax.ShapeDtypeStruct((m, kvw), F32)],
        compiler_params=_cparams("parallel"),
        name="q_sample",
    )(z, z, cos_t, sin_t, wts["g_cq"], wts["w_uq"], wts["g_q"], wts["g_kv"], wts["w_uk3"], wts["g_k"],
      wts["g_t1"], wts["g_t2"])


def _pattn_body(q_ref, k_ref, ct_ref, wuv_ref, y_ref, m_ref, l_ref, acc_ref, *, heads, tq, tk):
    qi = pl.program_id(1)
    ki = pl.program_id(2)

    @pl.when(ki == 0)
    def _():
        m_ref[...] = jnp.full(m_ref.shape, -jnp.inf, F32)
        l_ref[...] = jnp.zeros(l_ref.shape, F32)
        acc_ref[...] = jnp.zeros(acc_ref.shape, F32)

    def tile(masked):
        ct = ct_ref[...]
        if masked:
            kpos = ki * tk + lax.broadcasted_iota(jnp.int32, (tk, tq), 0)
            qpos = qi * tq + lax.broadcasted_iota(jnp.int32, (tk, tq), 1)
            visible = kpos <= qpos
        for h in range(heads):
            lo = h * HEAD_PAD
            s = _dot_nt(k_ref[:, lo:lo + HEAD_PAD], q_ref[:, lo:lo + HEAD_PAD])
            if masked:
                s = jnp.where(visible, s, -jnp.inf)
            m_prev = m_ref[h]
            m_new = jnp.maximum(m_prev, jnp.max(s, axis=0, keepdims=True))
            alpha = jnp.exp(m_prev - m_new)
            p = jnp.exp(s - m_new)
            l_ref[h] = alpha * l_ref[h] + jnp.sum(p, axis=0, keepdims=True)
            acc_ref[h] = alpha * acc_ref[h] + _dot(ct, p.astype(BF16))
            m_ref[h] = m_new

    first_q, last_q = qi * tq, qi * tq + tq - 1
    first_k, last_k = ki * tk, ki * tk + tk - 1

    @pl.when(last_k <= first_q)
    def _():
        tile(False)

    @pl.when((last_k > first_q) & (first_k <= last_q))
    def _():
        tile(True)

    @pl.when(ki == pl.num_programs(2) - 1)
    def _():
        for h in range(heads):
            lat = (acc_ref[h] / l_ref[h]).T.astype(BF16)
            y_ref[:, h * LANE:(h + 1) * LANE] = _dot(lat, wuv_ref[h]).astype(BF16)


def _prompt_attention(q, k, cb_t, w_uv, layer, batch, seq, heads, tq, tk):
    m = q.shape[0]
    nq, nk = seq // tq, seq // tk
    hp = heads * HEAD_PAD
    kvw = cb_t.shape[0]

    def kv_blk(b, i, j):
        return b * nk + jnp.minimum(j, (i * tq + tq - 1) // tk)

    return pl.pallas_call(
        functools.partial(_pattn_body, heads=heads, tq=tq, tk=tk),
        grid=(batch, nq, nk),
        in_specs=[pl.BlockSpec((tq, hp), lambda b, i, j: (b * nq + i, 0)),
                  pl.BlockSpec((tk, hp), lambda b, i, j: (kv_blk(b, i, j), 0)),
                  pl.BlockSpec((kvw, tk), lambda b, i, j: (0, kv_blk(b, i, j))),
                  pl.BlockSpec((None, heads, kvw, LANE), lambda b, i, j: (layer, 0, 0, 0))],
        out_specs=pl.BlockSpec((tq, heads * LANE), lambda b, i, j: (b * nq + i, 0)),
        out_shape=jax.ShapeDtypeStruct((m, heads * LANE), BF16),
        scratch_shapes=[pltpu.VMEM((heads, 1, tq), F32), pltpu.VMEM((heads, 1, tq), F32),
                        pltpu.VMEM((heads, kvw, tq), F32)],
        compiler_params=_cparams("parallel", "parallel", "arbitrary"),
        name="prompt_attention",
    )(q, k, cb_t, w_uv)


def _key_tile_scores(lhs_main, lhs_rope, cb, kr_t, cs_t, heads, width, scale):
    tk = cb.shape[0]
    nk = heads * LANE
    r = _dot_nt(lhs_main, cb)
    kn = r[:nk].reshape(heads, LANE, tk)
    ssq = jnp.sum(kn * kn, axis=1)
    feat = (jnp.concatenate([kr_t, kr_t], axis=0) * cs_t).astype(BF16)
    sq = kr_t * kr_t
    hi = sq.astype(BF16)
    lo = (sq - hi.astype(F32)).astype(BF16)
    r2 = _dot(lhs_rope, jnp.concatenate([feat, hi, lo], axis=0))
    nq = lhs_rope.shape[0] - 2 * SUBLANE
    rinv = lax.rsqrt((ssq + r2[nq:nq + heads]) * (1.0 / width) + NORM_EPS) * scale
    s = (r[nk:] + r2[:nq]).reshape(nq // heads, heads, tk) * rinv[None]
    return s.reshape(nq, tk)


def _sattn_body(pt_ref, wt_ref, qlat_ref, qg_ref, cn_ref, krn_ref, cs_ref, csn_ref, wuv_ref, ckv_hbm, ckr_hbm,
                y_ref, cbuf, kbuf, cb_ref, sems, *, layer, heads, width, lq, n_pages, page, tk):
    s_idx = pl.program_id(0)
    n_seq = pl.num_programs(0)
    nq = lq * heads
    scale = width ** -0.5
    ppt = tk // page
    n_tiles = n_pages // ppt
    kvw = cbuf.shape[-1]

    def tile_copies(seq, t, slot):
        cps = []
        for i in range(ppt):
            pg = pt_ref[seq, t * ppt + i]
            cps.append(pltpu.make_async_copy(ckv_hbm.at[layer, pg], cbuf.at[slot, t, pl.ds(i * page, page)],
                                             sems.at[0, slot, t]))
            cps.append(pltpu.make_async_copy(ckr_hbm.at[layer, pg], kbuf.at[slot, t, i], sems.at[1, slot, t]))
        return cps

    def start_tile(seq, t, slot):
        for cp in tile_copies(seq, t, slot):
            cp.start()

    def wait_tile(seq, t, slot):
        for cp in tile_copies(seq, t, slot):
            cp.wait()

    slot = s_idx % 2
    has_next = s_idx + 1 < n_seq

    @pl.when(s_idx == 0)
    def _():
        def body(t, carry):
            start_tile(0, t, 0)
            return carry
        lax.fori_loop(0, n_tiles, body, 0)

    lhs_main = jnp.concatenate([wt_ref[...], qlat_ref[...]], axis=0)
    tail_r = lax.broadcasted_iota(jnp.int32, (2 * SUBLANE, 2 * LANE), 0)
    tail_c = lax.broadcasted_iota(jnp.int32, (2 * SUBLANE, 2 * LANE), 1)
    ones_rows = jnp.where((tail_r < SUBLANE) & (tail_c >= LANE), 1.0, 0.0).astype(BF16)
    lhs_rope = jnp.concatenate([qg_ref[...], ones_rows], axis=0)

    def update(carry, s, cb):
        m_prev, l_prev, acc = carry
        m_new = jnp.maximum(m_prev, jnp.max(s, axis=-1, keepdims=True))
        alpha = jnp.exp(m_prev - m_new)
        p = jnp.exp(s - m_new)
        return (m_new, alpha * l_prev + jnp.sum(p, axis=-1, keepdims=True),
                alpha * acc + _dot(p.astype(BF16), cb))

    def tile_scores(t):
        cb = cbuf[slot, t].astype(BF16)
        cb_ref[t % 2] = cb
        kr_t = jnp.concatenate([kbuf[slot, t, i] for i in range(ppt)], axis=1)
        return _key_tile_scores(lhs_main, lhs_rope, cb, kr_t, cs_ref[t], heads, width, scale)

    def prefetch_next(t):
        @pl.when(has_next)
        def _():
            start_tile(s_idx + 1, t, 1 - slot)

    def body(t, state):
        s_cur, carry = state
        prefetch_next(t)
        wait_tile(s_idx, t + 1, slot)
        s_next = tile_scores(t + 1)
        return s_next, update(carry, s_cur, cb_ref[t % 2])

    wait_tile(s_idx, 0, slot)
    carry = (jnp.full((nq, 1), -jnp.inf, F32), jnp.zeros((nq, 1), F32), jnp.zeros((nq, kvw), F32))
    s_last, carry = lax.fori_loop(0, n_tiles - 1, body, (tile_scores(0), carry))
    prefetch_next(n_tiles - 1)
    carry = update(carry, s_last, cb_ref[(n_tiles - 1) % 2])

    pad = LANE - cn_ref.shape[0]
    cb_new = jnp.concatenate([cn_ref[...], jnp.zeros((pad, kvw), F32)], axis=0).astype(BF16)
    s_new = _key_tile_scores(lhs_main, lhs_rope, cb_new, krn_ref[...], csn_ref[...], heads, width, scale)
    key_j = lax.broadcasted_iota(jnp.int32, (nq, LANE), 1)
    q_i = lax.broadcasted_iota(jnp.int32, (nq, LANE), 0) // heads
    s_new = jnp.where((key_j < lq) & (key_j <= q_i), s_new, -jnp.inf)
    _, l_fin, acc = update(carry, s_new, cb_new)

    lat = (acc / l_fin).astype(BF16)
    res = _dot(lat, wuv_ref[...]).reshape(lq, heads, heads * LANE)
    row_h = lax.broadcasted_iota(jnp.int32, (heads, heads * LANE), 0)
    col_h = lax.broadcasted_iota(jnp.int32, (heads, heads * LANE), 1) // LANE
    y_ref[...] = jnp.sum(jnp.where((row_h == col_h)[None], res, 0.0), axis=1)


def _sample_attention(page_table, wt, qlat, qg, c_new, kr_new_t, cs_t, cs_new_t, wuv, cache_kv, cache_kr_t,
                      layer, heads, width, lq, tk):
    n_seq, n_pages = page_table.shape
    page = cache_kv.shape[2]
    kvw = cache_kv.shape[3]
    krw = cache_kr_t.shape[2]
    n_tiles = n_pages * page // tk
    nq = lq * heads
    rows_new = c_new.shape[1]
    assert cache_kr_t.shape[3] == page and tk % page == 0 and page == LANE
    grid_spec = pltpu.PrefetchScalarGridSpec(
        num_scalar_prefetch=1,
        grid=(n_seq,),
        in_specs=[
            pl.BlockSpec((None,) + wt.shape[1:], lambda s, pt: (layer, 0, 0)),
            pl.BlockSpec((None, nq, kvw), lambda s, pt: (s, 0, 0)),
            pl.BlockSpec((None, nq, 2 * LANE), lambda s, pt: (s, 0, 0)),
            pl.BlockSpec((None, rows_new, kvw), lambda s, pt: (s, 0, 0)),
            pl.BlockSpec((None, krw, LANE), lambda s, pt: (s, 0, 0)),
            pl.BlockSpec(cs_t.shape, lambda s, pt: (0, 0, 0)),
            pl.BlockSpec(cs_new_t.shape, lambda s, pt: (0, 0)),
            pl.BlockSpec((None,) + wuv.shape[1:], lambda s, pt: (layer, 0, 0)),
            pl.BlockSpec(memory_space=pl.ANY),
            pl.BlockSpec(memory_space=pl.ANY),
        ],
        out_specs=pl.BlockSpec((None, lq, heads * LANE), lambda s, pt: (s, 0, 0)),
        scratch_shapes=[pltpu.VMEM((2, n_tiles, tk, kvw), F32), pltpu.VMEM((2, n_tiles, tk // page, krw, page), F32),
                        pltpu.VMEM((2, tk, kvw), BF16), pltpu.SemaphoreType.DMA((2, 2, n_tiles))],
    )
    return pl.pallas_call(
        functools.partial(_sattn_body, layer=layer, heads=heads, width=width, lq=lq, n_pages=n_pages,
                          page=page, tk=tk),
        grid_spec=grid_spec,
        out_shape=jax.ShapeDtypeStruct((n_seq, lq, heads * LANE), F32),
        compiler_params=_cparams("arbitrary"),
        name="sample_attention",
    )(page_table, wt, qlat, qg, c_new, kr_new_t, cs_t, cs_new_t, wuv, cache_kv, cache_kr_t)


def _ln_swish(yc, g, b):
    mu = jnp.mean(yc, axis=-1, keepdims=True)
    xc = yc - mu
    y = xc * lax.rsqrt(jnp.mean(xc * xc, axis=-1, keepdims=True) + LN_EPS) * g + b
    return _silu(y)


def _conv_prompt_body(a_ref, gate_ref, w_ref, b_ref, lg_ref, lb_ref, act_ref, new_ref, xp_ref, slab_ref, *,
                      taps, chunk):
    seq = a_ref.shape[0]
    front = xp_ref.shape[0] - seq
    shift = front - (taps - 1)
    xp_ref[0:front, :] = jnp.zeros((front, xp_ref.shape[1]), F32)
    xp_ref[front:, :] = a_ref[...] * _sigmoid(gate_ref[...])
    new_ref[...] = xp_ref[seq + shift:seq + front, :]

    def body(i, carry):
        r0 = pl.multiple_of(i * chunk, chunk)
        slab_ref[0] = xp_ref[pl.ds(r0, chunk + front), :]
        keep = chunk + front - SUBLANE
        for p in range(1, SUBLANE):
            slab_ref[p, 0:keep, :] = slab_ref[0, p:p + keep, :]
        acc = jnp.zeros((chunk, xp_ref.shape[1]), F32) + b_ref[...]
        for j in range(taps):
            p = (shift + j) % SUBLANE
            base = shift + j - p
            acc = acc + slab_ref[p, base:base + chunk, :] * w_ref[j:j + 1, :]
        act_ref[pl.ds(r0, chunk), :] = _ln_swish(acc, lg_ref[...], lb_ref[...]).astype(BF16)
        return carry

    lax.fori_loop(0, seq // chunk, body, 0)


def _conv_prompt(z, cols, wts, layer, batch, seq, taps):
    ch = cols["glu_a"][1]
    a_off = cols["glu_a"][0] // ch
    g_off = cols["glu_g"][0] // ch
    front = -(-(taps - 1) // SUBLANE) * SUBLANE

    def wspec(a):
        return pl.BlockSpec((None,) + a.shape[1:], lambda b: (layer,) + (0,) * (a.ndim - 1))

    chunk = _tile(seq, 64)
    return pl.pallas_call(
        functools.partial(_conv_prompt_body, taps=taps, chunk=chunk),
        grid=(batch,),
        in_specs=[pl.BlockSpec((seq, ch), lambda b: (b, a_off)), pl.BlockSpec((seq, ch), lambda b: (b, g_off)),
                  wspec(wts["conv_w"]), wspec(wts["conv_b"]), wspec(wts["ln_g"]), wspec(wts["ln_b"])],
        out_specs=[pl.BlockSpec((seq, ch), lambda b: (b, 0)),
                   pl.BlockSpec((None, taps - 1, ch), lambda b: (b, 0, 0))],
        out_shape=[jax.ShapeDtypeStruct((batch * seq, ch), BF16),
                   jax.ShapeDtypeStruct((batch, taps - 1, ch), F32)],
        scratch_shapes=[pltpu.VMEM((front + seq, ch), F32), pltpu.VMEM((SUBLANE, front + chunk, ch), F32)],
        compiler_params=_cparams("parallel"),
        name="conv_prompt",
    )(z, z, wts["conv_w"], wts["conv_b"], wts["ln_g"], wts["ln_b"])


def _conv_sample_body(a_ref, gate_ref, st_ref, w_ref, b_ref, lg_ref, lb_ref, act_ref, new_ref, xp_ref,
                      u_ref, yc_ref, *, taps, lq, nseq):
    hist = taps - 1
    u_ref[...] = a_ref[...] * _sigmoid(gate_ref[...])
    w = w_ref[...]
    for s in range(nseq):
        xp_ref[0:hist, :] = st_ref[s]
        xp_ref[hist:hist + lq, :] = u_ref[s * lq:(s + 1) * lq, :]
        new_ref[s] = xp_ref[lq:lq + hist, :]
        for t in range(lq):
            yc_ref[s * lq + t:s * lq + t + 1, :] = jnp.sum(xp_ref[t:t + taps, :] * w, axis=0, keepdims=True)
    act_ref[...] = _ln_swish(yc_ref[...] + b_ref[...], lg_ref[...], lb_ref[...]).astype(BF16)


def _conv_sample(z, state, cols, wts, layer, lq, taps, nseq):
    ch = cols["glu_a"][1]
    a_off = cols["glu_a"][0] // ch
    g_off = cols["glu_g"][0] // ch
    n = state.shape[1]
    rows = nseq * lq

    def wspec(a):
        return pl.BlockSpec((None,) + a.shape[1:], lambda b: (layer,) + (0,) * (a.ndim - 1))

    return pl.pallas_call(
        functools.partial(_conv_sample_body, taps=taps, lq=lq, nseq=nseq),
        grid=(n // nseq,),
        in_specs=[pl.BlockSpec((rows, ch), lambda b: (b, a_off)), pl.BlockSpec((rows, ch), lambda b: (b, g_off)),
                  pl.BlockSpec((None, nseq, taps - 1, ch), lambda b: (layer, b, 0, 0)),
                  wspec(wts["conv_w"]), wspec(wts["conv_b"]), wspec(wts["ln_g"]), wspec(wts["ln_b"])],
        out_specs=[pl.BlockSpec((rows, ch), lambda b: (b, 0)),
                   pl.BlockSpec((nseq, taps - 1, ch), lambda b: (b, 0, 0))],
        out_shape=[jax.ShapeDtypeStruct((n * lq, ch), BF16),
                   jax.ShapeDtypeStruct((n, taps - 1, ch), F32)],
        scratch_shapes=[pltpu.VMEM((-(-(taps - 1 + lq) // SUBLANE) * SUBLANE, ch), F32),
                        pltpu.VMEM((rows, ch), F32), pltpu.VMEM((rows, ch), F32)],
        compiler_params=_cparams("parallel"),
        name="conv_sample",
    )(z, z, state, wts["conv_w"], wts["conv_b"], wts["ln_g"], wts["ln_b"])


def _log_sigmoid(x):
    return jnp.minimum(x, 0.0) - jnp.log1p(jnp.exp(-jnp.abs(x)))


def _split3(x):
    h1 = x.astype(BF16)
    r1 = x - h1.astype(F32)
    h2 = r1.astype(BF16)
    h3 = (r1 - h2.astype(F32)).astype(BF16)
    return h1, h2, h3


def _gla_body(gq_ref, gk_ref, gv_ref, ga_ref, gr_ref, s0_ref, wa_ref, ba_ref, gn_ref, y_ref, sout_ref,
              st_ref, oi_ref, *, heads, dk, dv, valid, carried):
    T = GLA_TILE
    R = GLA_BLOCK
    nb = T // R
    pairs = heads // 2
    hk = heads * dk
    ti = pl.program_id(1)

    if carried:
        @pl.when(ti == 0)
        def _():
            st_ref[...] = jnp.zeros(st_ref.shape, F32)

    row = lax.broadcasted_iota(jnp.int32, (T, T), 0)
    col = lax.broadcasted_iota(jnp.int32, (T, T), 1)
    same = (row // R) == (col // R)
    tri = jnp.where(same & (col <= row), 1.0, 0.0).astype(BF16)
    ones_blk = jnp.where(same, 1.0, 0.0).astype(BF16)

    la = _log_sigmoid(_dot(ga_ref[...].astype(BF16), wa_ref[...]) + ba_ref[...]) * (1.0 / GLA_TAU)
    if valid < R:
        rr = lax.broadcasted_iota(jnp.int32, (T, hk), 0) % R
        la = jnp.where(rr < valid, la, 0.0)
    parts = jnp.concatenate(_split3(la), axis=-1)
    cums = _dot(jnp.concatenate([tri, ones_blk], axis=0), parts)
    bc = cums[:T, :hk] + cums[:T, hk:2 * hk] + cums[:T, 2 * hk:]
    bl = cums[T:, :hk] + cums[T:, hk:2 * hk] + cums[T:, 2 * hk:]
    q = gq_ref[...] * (dk ** -0.5)
    k = gk_ref[...]
    v = gv_ref[...]
    qe = q * jnp.exp(bc)
    kd = k * jnp.exp(bl - bc)
    dec = jnp.exp(bl)

    hd_row = lax.broadcasted_iota(jnp.int32, (hk, heads * dv), 0) // dk
    hd_col = lax.broadcasted_iota(jnp.int32, (hk, heads * dv), 1) // dv
    expand = jnp.where(hd_row == hd_col, 1.0, 0.0).astype(BF16)
    jrow = lax.broadcasted_iota(jnp.int32, (R, hk), 0)
    for t in range(nb):
        rs = slice(t * R, (t + 1) * R)
        b, qb, kb, vb = bc[rs], q[rs], k[rs], v[rs]
        p = []
        for i in range(R):
            e = jnp.exp(jnp.where(jrow <= i, b[i:i + 1, :] - b, -jnp.inf))
            p.append((e * qb[i:i + 1, :] * kb).astype(BF16))
        a = _dot(jnp.concatenate(p, axis=0), expand)
        for i in range(R):
            oi_ref[t * R + i:t * R + i + 1, :] = jnp.sum(a[i * R:(i + 1) * R] * vb, axis=0, keepdims=True)
    o = oi_ref[...]

    lane = lax.broadcasted_iota(jnp.int32, (T, 2 * dk), 1)
    blk = lax.broadcasted_iota(jnp.int32, (T, 2 * dk), 0) // R
    lane_blk = lax.broadcasted_iota(jnp.int32, (R, 2 * dk), 1)
    o_cols = []
    for pr in range(pairs):
        ls = slice(pr * 2 * dk, (pr + 1) * 2 * dk)
        kd_p, qe_p, dec_p = kd[:, ls], qe[:, ls], dec[:, ls]
        kv = None
        for hh in range(2):
            h = 2 * pr + hh
            kd_h = jnp.where((lane // dk) == hh, kd_p, 0.0)
            stack = jnp.concatenate([jnp.where(blk == t, kd_h, 0.0) for t in range(nb)], axis=-1).astype(BF16)
            vt = v[:, h * dv:(h + 1) * dv].T.astype(BF16)
            part = _dot(vt, stack)
            kv = part if kv is None else kv + part
        o_pair = [[], []]
        for t in range(nb):
            rs = slice(t * R, (t + 1) * R)
            if carried:
                st = st_ref[pr]
            else:
                s0 = s0_ref[t]
                st = jnp.concatenate([s0[2 * pr], s0[2 * pr + 1]], axis=0).T
            stb = st.astype(BF16)
            for hh in range(2):
                qm = jnp.where((lane_blk // dk) == hh, qe_p[rs], 0.0).astype(BF16)
                o_pair[hh].append(_dot_nt(qm, stb))
            st = st * dec_p[t * R:t * R + 1, :] + kv[:, t * 2 * dk:(t + 1) * 2 * dk]
            if carried:
                st_ref[pr] = st
            else:
                stt = st.T
                sout_ref[t, 2 * pr] = stt[:dk]
                sout_ref[t, 2 * pr + 1] = stt[dk:]
        for hh in range(2):
            o_cols.append(jnp.concatenate(o_pair[hh], axis=0))
    o = o + jnp.concatenate(o_cols, axis=-1)

    gr = gr_ref[...]
    for h in range(heads):
        cs = slice(h * dv, (h + 1) * dv)
        y_ref[:, cs] = (_rms(o[:, cs], gn_ref[...]) * _silu(gr[:, cs])).astype(BF16)

    if carried:
        @pl.when(ti == pl.num_programs(1) - 1)
        def _():
            for pr in range(pairs):
                stt = st_ref[pr].T
                sout_ref[2 * pr] = stt[:dk]
                sout_ref[2 * pr + 1] = stt[dk:]


def _gla(zg, cols, s0, wts, layer, batch, seq, heads, dk, dv, valid, carried):
    T = GLA_TILE
    m = zg.shape[0]
    hk, hv = heads * dk, heads * dv
    if carried:
        grid = (batch, seq // T)
        rowblk = lambda b, t: b * (seq // T) + t
        s_spec = pl.BlockSpec((None, heads, dk, dv), lambda b, t: (b, 0, 0, 0))
        s0_arr = jnp.zeros((1, heads, dk, dv), F32)
        s0_spec = pl.BlockSpec((1, heads, dk, dv), lambda b, t: (0, 0, 0, 0))
        n_state = batch
    else:
        nb = T // GLA_BLOCK
        grid = (1, m // T)
        rowblk = lambda b, t: t
        s_spec = pl.BlockSpec((nb, heads, dk, dv), lambda b, t: (t, 0, 0, 0))
        s0_arr = s0
        s0_spec = pl.BlockSpec((None, nb, heads, dk, dv), lambda b, t: (layer, t, 0, 0, 0))
        n_state = m // GLA_BLOCK

    def zspec(name):
        off, w = cols[name]
        return pl.BlockSpec((T, w), lambda b, t: (rowblk(b, t), off // w))

    def wspec(a):
        return pl.BlockSpec((None,) + a.shape[1:], lambda b, t: (layer,) + (0,) * (a.ndim - 1))

    return pl.pallas_call(
        functools.partial(_gla_body, heads=heads, dk=dk, dv=dv, valid=valid, carried=carried),
        grid=grid,
        in_specs=[zspec("gq"), zspec("gk"), zspec("gv"), zspec("ga"), zspec("gr"), s0_spec,
                  wspec(wts["gla_wa"]), wspec(wts["gla_ba"]), wspec(wts["gla_norm"])],
        out_specs=[pl.BlockSpec((T, hv), lambda b, t: (rowblk(b, t), 0)), s_spec],
        out_shape=[jax.ShapeDtypeStruct((m, hv), BF16), jax.ShapeDtypeStruct((n_state, heads, dk, dv), F32)],
        scratch_shapes=[pltpu.VMEM((heads // 2, dv, 2 * dk), F32), pltpu.VMEM((T, hv), F32)],
        compiler_params=_cparams("parallel", "arbitrary"),
        name="gla_carried" if carried else "gla_blocks",
    )(zg, zg, zg, zg, zg, s0_arr, wts["gla_wa"], wts["gla_ba"], wts["gla_norm"])


def _merge_body(ya_ref, yb_ref, yc_ref, g0_ref, g1_ref, g2_ref, wpa_ref, wpb_ref, wpc_ref, m_ref):
    m = _sigmoid(g0_ref[...]) * _dot(ya_ref[...], wpa_ref[...])
    m = m + _sigmoid(g1_ref[...]) * _dot(yb_ref[...], wpb_ref[...])
    m = m + _sigmoid(g2_ref[...]) * _dot(yc_ref[...], wpc_ref[...])
    m_ref[...] = m.astype(BF16)


def _merge(ya, yb, yc, z, cols, wts, layer, tm):
    m = ya.shape[0]
    d = wts["w_pa"].shape[-1]
    mg_off = cols["mg"][0] // d

    def wspec(a):
        return pl.BlockSpec((None,) + a.shape[1:], lambda i: (layer,) + (0,) * (a.ndim - 1))

    def gspec(idx):
        return pl.BlockSpec((tm, d), lambda i: (i, mg_off + idx))

    def yspec(a):
        return pl.BlockSpec((tm, a.shape[1]), lambda i: (i, 0))

    return pl.pallas_call(
        _merge_body,
        grid=(m // tm,),
        in_specs=[yspec(ya), yspec(yb), yspec(yc), gspec(0), gspec(1), gspec(2),
                  wspec(wts["w_pa"]), wspec(wts["w_pb"]), wspec(wts["w_pc"])],
        out_specs=pl.BlockSpec((tm, d), lambda i: (i, 0)),
        out_shape=jax.ShapeDtypeStruct((m, d), BF16),
        compiler_params=_cparams("parallel"),
        name="merge",
    )(ya, yb, yc, z, z, z, wts["w_pa"], wts["w_pb"], wts["w_pc"])


def _out_proj_body(x_ref, m_ref, w_ref, o_ref):
    o_ref[...] = x_ref[...] + _dot(m_ref[...], w_ref[...])


def _out_proj(x, mm, w, layer, tm, tn):
    m, d = x.shape
    return pl.pallas_call(
        _out_proj_body,
        grid=(m // tm, d // tn),
        in_specs=[pl.BlockSpec((tm, tn), lambda i, j: (i, j)),
                  pl.BlockSpec((tm, d), lambda i, j: (i, 0)),
                  pl.BlockSpec((None, d, tn), lambda i, j: (layer, 0, j))],
        out_specs=pl.BlockSpec((tm, tn), lambda i, j: (i, j)),
        out_shape=jax.ShapeDtypeStruct((m, d), F32),
        compiler_params=_cparams("parallel", "parallel"),
        name="out_proj",
    )(x, mm, w)


def _ffn_body(x_ref, g_ref, wg_ref, wu_ref, wo_ref, o_ref, h_ref, acc_ref):
    f = pl.program_id(1)

    @pl.when(f == 0)
    def _():
        h_ref[...] = _rms(x_ref[...], g_ref[...]).astype(BF16)
        acc_ref[...] = jnp.zeros(acc_ref.shape, F32)

    h = h_ref[...]
    act = (_silu(_dot(h, wg_ref[...])) * _dot(h, wu_ref[...])).astype(BF16)
    acc_ref[...] += _dot(act, wo_ref[...])

    @pl.when(f == pl.num_programs(1) - 1)
    def _():
        o_ref[...] = x_ref[...] + acc_ref[...]


def _ffn(x, g, w_in, w_out, layer, tm, tf):
    m, d = x.shape
    dff = w_out.shape[1]
    nf = dff // tf
    return pl.pallas_call(
        _ffn_body,
        grid=(m // tm, nf),
        in_specs=[pl.BlockSpec((tm, d), lambda i, f: (i, 0)),
                  pl.BlockSpec((None, 1, d), lambda i, f: (layer, 0, 0)),
                  pl.BlockSpec((None, d, tf), lambda i, f: (layer, 0, f)),
                  pl.BlockSpec((None, d, tf), lambda i, f: (layer, 0, nf + f)),
                  pl.BlockSpec((None, tf, d), lambda i, f: (layer, f, 0))],
        out_specs=pl.BlockSpec((tm, d), lambda i, f: (i, 0)),
        out_shape=jax.ShapeDtypeStruct((m, d), F32),
        scratch_shapes=[pltpu.VMEM((tm, d), BF16), pltpu.VMEM((tm, d), F32)],
        compiler_params=_cparams("parallel", "arbitrary"),
        name="ffn",
    )(x, g, w_in, w_in, w_out)


def _rope_angles(pos):
    half = 32
    inv = jnp.exp(-math.log(ROPE_BASE) * jnp.arange(half, dtype=F32) / half)
    ang = pos.astype(F32)[:, None] * inv[None, :]
    return jnp.cos(ang), jnp.sin(ang)


def _rope_tables(pos):
    cos, sin = _rope_angles(pos)
    z = jnp.zeros((pos.shape[0], LANE // 2), F32)
    return jnp.concatenate([cos, cos, z], -1), jnp.concatenate([-sin, sin, z], -1)


def _pad_cols(a, width):
    return jnp.pad(a, [(0, 0)] * (a.ndim - 1) + [(0, width - a.shape[-1])])


def kernel(x_prompt, x_sample, cache_kv, cache_krope, state_conv, state_gla, page_table, attn_norm, w_in, cq_norm, w_uq, q_norm, kv_norm, w_uk, k_norm, w_uv, w_pa, conv_w, conv_b, conv_ln_g, conv_ln_b, w_pb, gla_wa, gla_ba, gla_norm, w_pc, w_o, ffn_norm, w_ffn_in, w_ffn_out):
    depth, d_model, _ = w_in.shape
    batch, seq, _ = x_prompt.shape
    n_seq, lq, _ = x_sample.shape
    q_lora = cq_norm.shape[1]
    kv_lora = kv_norm.shape[1]
    heads, qk_head = w_uq.shape[2], w_uq.shape[3]
    qk_nope = w_uk.shape[3]
    qk_rope = qk_head - qk_nope
    conv_ch = conv_b.shape[1]
    taps = conv_w.shape[1]
    g_heads, g_dk, g_dv = state_gla.shape[2], state_gla.shape[3], state_gla.shape[4]
    g_rank = gla_wa.shape[1]
    n_pages, page = page_table.shape[1], cache_kv.shape[2]
    past = n_pages * page
    assert (qk_nope, qk_rope, kv_lora) == (LANE, LANE // 2, 2 * LANE) and g_heads % 2 == 0 and 2 * g_dk == LANE

    ref_sizes = dict(cq=q_lora, ckv=kv_lora, kr=qk_rope, glu=2 * conv_ch, gq=g_heads * g_dk, gk=g_heads * g_dk,
                     gv=g_heads * g_dv, ga=g_rank, gr=g_heads * g_dv, mg=3 * d_model)
    ref_off, off = {}, 0
    for name, size in ref_sizes.items():
        ref_off[name] = off
        off += size
    pieces = [("mg", ref_off["mg"], 3 * d_model, 3 * d_model),
              ("glu_a", ref_off["glu"], conv_ch, conv_ch), ("glu_g", ref_off["glu"] + conv_ch, conv_ch, conv_ch),
              ("cq", ref_off["cq"], q_lora, q_lora), ("gv", ref_off["gv"], g_heads * g_dv, g_heads * g_dv),
              ("gr", ref_off["gr"], g_heads * g_dv, g_heads * g_dv), ("ckv", ref_off["ckv"], kv_lora, kv_lora),
              ("gq", ref_off["gq"], g_heads * g_dk, g_heads * g_dk), ("gk", ref_off["gk"], g_heads * g_dk, g_heads * g_dk),
              ("kr", ref_off["kr"], qk_rope, LANE), ("ga", ref_off["ga"], g_rank, LANE)]
    cols, off, packed = {}, 0, []
    for name, src, size, width in pieces:
        cols[name] = (off, width)
        packed.append(_pad_cols(w_in[:, :, src:src + size], width))
        off += width
    cols["mg"] = (0, d_model)
    w_in_p = jnp.concatenate(packed, axis=-1).astype(BF16)
    n_in = w_in_p.shape[-1]

    wuq_p = jnp.concatenate([w_uq[..., :qk_nope], _pad_cols(w_uq[..., qk_nope:], LANE)], axis=-1)
    wuq_p = wuq_p.reshape(depth, q_lora, heads * HEAD_PAD).astype(BF16)
    pad_gain = lambda g: jnp.concatenate([g[:, :qk_nope], _pad_cols(g[:, qk_nope:], LANE)], -1)[:, None, :]
    half = qk_rope // 2
    g_r1, g_r2 = k_norm[:, qk_nope:qk_nope + half], k_norm[:, qk_nope + half:]
    zeros_h = jnp.zeros_like(g_r1)
    attn_w = dict(
        g_cq=cq_norm[:, None, :], w_uq=wuq_p, g_q=pad_gain(q_norm), g_kv=kv_norm[:, None, :],
        w_uk=w_uk.reshape(depth, kv_lora, heads * qk_nope).astype(BF16), g_k=pad_gain(k_norm),
        w_uk3=jnp.transpose(w_uk, (0, 2, 1, 3)).astype(BF16),
        g_t1=jnp.concatenate([g_r1, g_r2, zeros_h, zeros_h], -1)[:, None, :],
        g_t2=jnp.concatenate([zeros_h, zeros_h, g_r1, -g_r2], -1)[:, None, :],
    )
    w_ukT = jnp.transpose(w_uk.reshape(depth, kv_lora, heads * qk_nope), (0, 2, 1)).astype(BF16)
    w_uv_heads = jnp.transpose(w_uv, (0, 2, 1, 3)).astype(BF16)
    w_uv_flat = w_uv.reshape(depth, kv_lora, heads * w_uv.shape[3]).astype(BF16)
    conv_wts = dict(conv_w=conv_w, conv_b=conv_b[:, None, :], ln_g=conv_ln_g[:, None, :], ln_b=conv_ln_b[:, None, :])
    gla_wts = dict(gla_wa=jnp.pad(gla_wa, ((0, 0), (0, LANE - g_rank), (0, 0))).astype(BF16),
                   gla_ba=gla_ba[:, None, :], gla_norm=gla_norm[:, None, :])
    merge_wts = dict(w_pa=w_pa.astype(BF16), w_pb=w_pb.astype(BF16), w_pc=w_pc.astype(BF16))
    w_o_b = w_o.astype(BF16)
    w_ffn_in_b = w_ffn_in.astype(BF16)
    w_ffn_out_b = w_ffn_out.astype(BF16)
    g_attn = attn_norm[:, None, :]
    g_ffn = ffn_norm[:, None, :]

    cos_p, sin_p = _rope_tables(jnp.arange(seq, dtype=jnp.int32))
    cos_s, sin_s = _rope_tables(past + jnp.arange(lq, dtype=jnp.int32))
    cos_s, sin_s = jnp.tile(cos_s, (n_seq, 1)), jnp.tile(sin_s, (n_seq, 1))
    tk_s = _tile(past, 2048)
    kc, ks = _rope_angles(jnp.arange(past + LANE, dtype=jnp.int32))
    key_cs_t = jnp.concatenate([kc, kc, ks, ks], axis=-1).T
    cs_past_t = jnp.transpose(key_cs_t[:, :past].reshape(2 * qk_rope, past // tk_s, tk_s), (1, 0, 2))
    cs_new_t = key_cs_t[:, past:]
    cache_kr_t = jnp.swapaxes(cache_krope, 2, 3)

    mp, ms = batch * seq, n_seq * lq
    tm_p = _tile(mp, 1024)
    tn_in = _tile(n_in, 512)
    tf = _tile(w_ffn_out.shape[1], 512)
    gla_cols = {k: cols[k] for k in ("gq", "gk", "gv", "ga", "gr")}
    gla_lo = min(v[0] for v in gla_cols.values())

    xp = x_prompt.reshape(mp, d_model)
    xs = x_sample.reshape(ms, d_model)
    outs = {k: [] for k in ("ckv_p", "kr_p", "conv_p", "gla_p", "ckv_s", "kr_s", "conv_s", "gla_s")}
    kr_off = cols["kr"][0]
    for l in range(depth):
        z = _norm_matmul(xp, g_attn, w_in_p, l, tm_p, tn_in)
        q, k, ckv, cb_t = _qk_prompt(z, cols, cos_p, sin_p, attn_w, l, _tile(seq, 256), heads, qk_head)
        ya = _prompt_attention(q, k, cb_t, w_uv_heads, l, batch, seq, heads, _tile(seq, 512), _tile(seq, 512))
        yb, conv_new = _conv_prompt(z, cols, conv_wts, l, batch, seq, taps)
        yc, gla_new = _gla(z, gla_cols, None, gla_wts, l, batch, seq, g_heads, g_dk, g_dv, GLA_BLOCK, True)
        mm = _merge(ya, yb, yc, z, cols, merge_wts, l, _tile(mp, 256))
        x1 = _out_proj(xp, mm, w_o_b, l, tm_p, _tile(d_model, 512))
        xp = _ffn(x1, g_ffn, w_ffn_in_b, w_ffn_out_b, l, _tile(mp, 512), tf)
        outs["ckv_p"].append(ckv.reshape(batch, seq, kv_lora))
        outs["kr_p"].append(z[:, kr_off:kr_off + qk_rope].reshape(batch, seq, qk_rope))
        outs["conv_p"].append(conv_new)
        outs["gla_p"].append(gla_new)

        z = _norm_matmul(xs, g_attn, w_in_p, l, ms, tn_in)
        qlat, qg, ckv = _q_sample(z, cols, cos_s, sin_s, attn_w, l, heads, qk_head)
        kr_new = z[:, kr_off:kr_off + qk_rope]
        rows_new = -(-lq // SUBLANE) * SUBLANE
        c_new = jnp.pad(ckv.reshape(n_seq, lq, kv_lora), ((0, 0), (0, rows_new - lq), (0, 0)))
        kr_new_t = jnp.pad(jnp.swapaxes(kr_new.reshape(n_seq, lq, qk_rope), 1, 2), ((0, 0), (0, 0), (0, LANE - lq)))
        ya = _sample_attention(page_table, w_ukT, qlat.reshape(n_seq, lq * heads, kv_lora),
                               qg.reshape(n_seq, lq * heads, 2 * LANE), c_new, kr_new_t, cs_past_t, cs_new_t,
                               w_uv_flat, cache_kv, cache_kr_t, l, heads, qk_head, lq, tk_s)
        ya = ya.reshape(ms, heads * LANE).astype(BF16)
        yb, conv_new = _conv_sample(z, state_conv, cols, conv_wts, l, lq, taps, SUBLANE)
        zg = z[:, gla_lo:]
        zg = jnp.pad(zg.reshape(n_seq, lq, -1), ((0, 0), (0, GLA_BLOCK - lq), (0, 0))).reshape(n_seq * GLA_BLOCK, -1)
        zg_cols = {k: (v[0] - gla_lo, v[1]) for k, v in gla_cols.items()}
        yc, gla_new = _gla(zg, zg_cols, state_gla, gla_wts, l, 1, GLA_BLOCK, g_heads, g_dk, g_dv, lq, False)
        yc = yc.reshape(n_seq, GLA_BLOCK, -1)[:, :lq].reshape(ms, -1)
        mm = _merge(ya, yb, yc, z, cols, merge_wts, l, _tile(ms, 256))
        x1 = _out_proj(xs, mm, w_o_b, l, ms, _tile(d_model, 512))
        xs = _ffn(x1, g_ffn, w_ffn_in_b, w_ffn_out_b, l, ms, tf)
        outs["ckv_s"].append(ckv.reshape(n_seq, lq, kv_lora))
        outs["kr_s"].append(kr_new.reshape(n_seq, lq, qk_rope))
        outs["conv_s"].append(conv_new)
        outs["gla_s"].append(gla_new)

    stack = lambda key: jnp.stack(outs[key])
    return (xp.reshape(batch, seq, d_model), xs.reshape(n_seq, lq, d_model),
            stack("ckv_p"), stack("kr_p"), stack("conv_p"), stack("gla_p"),
            stack("ckv_s"), stack("kr_s"), stack("conv_s"), stack("gla_s"))
```
